```python
import jax, jax.numpy as jnp
from jax import lax
import numpy as np

D_MODEL = 1024
BATCH = 1
SEQ = 16384
DEPTH = 4

HEAD_DIM = 64
N_HEADS = D_MODEL // HEAD_DIM
N_HEADS_A = N_HEADS // 2
N_HEADS_B = N_HEADS - N_HEADS_A
N_HEADS_C = N_HEADS
IDX_HEADS = 16
IDX_DIM = 64
TOPK_MAX = 256
DIL_PATTERNS = ((128, 1), (512, 4), (2048, 16))
Q_BLOCK = 128
D_FF = 2816
CONV_WIDTH = 3
P_DIM = 256
ROPE_THETA = 10000.0
EPS = 1e-6
N_EVEN = (DEPTH + 1) // 2
N_ODD = DEPTH // 2
EVEN_SPLITS = (N_HEADS_A * HEAD_DIM,) * 3 + (N_HEADS_A,) + (N_HEADS_B * HEAD_DIM,) * 3 + (IDX_HEADS * IDX_DIM, IDX_DIM, IDX_HEADS)
WIN_EVEN = int(sum(EVEN_SPLITS))
WIN_ODD = 3 * N_HEADS_C * HEAD_DIM

kernel_name = "fox_dsa_dilated_hybrid_trunk"


def rmsnorm(x, g):
    xf = x.astype(jnp.float32)
    y = xf * lax.rsqrt(jnp.mean(xf * xf, axis=-1, keepdims=True) + EPS)
    return (y * g.astype(jnp.float32)).astype(x.dtype)


def rope(x, pos):
    half = x.shape[-1] // 2
    inv_freq = ROPE_THETA ** (-jnp.arange(half, dtype=jnp.float32) / half)
    ang = pos.astype(jnp.float32)[:, None] * inv_freq[None, :]
    cos = jnp.cos(ang)[None, :, None, :]
    sin = jnp.sin(ang)[None, :, None, :]
    xf = x.astype(jnp.float32)
    x1, x2 = xf[..., :half], xf[..., half:]
    return jnp.concatenate([x1 * cos - x2 * sin, x2 * cos + x1 * sin], axis=-1).astype(x.dtype)


def fox_attention(q, k, v, log_f):
    B, S, H, D = q.shape
    scale = D ** -0.5
    c = jnp.cumsum(log_f, axis=1).transpose(0, 2, 1)
    kpos = jnp.arange(S)

    def block(i):
        t0 = i * Q_BLOCK
        qb = lax.dynamic_slice_in_dim(q, t0, Q_BLOCK, axis=1)
        cb = lax.dynamic_slice_in_dim(c, t0, Q_BLOCK, axis=2)
        s = jnp.einsum('bqhd,bkhd->bhqk', qb, k, preferred_element_type=jnp.float32) * scale
        s = s + cb[..., :, None] - c[..., None, :]
        qpos = t0 + jnp.arange(Q_BLOCK)
        causal = kpos[None, :] <= qpos[:, None]
        prob = jax.nn.softmax(jnp.where(causal[None, None], s, -jnp.inf), axis=-1)
        return jnp.einsum('bhqk,bkhd->bqhd', prob.astype(v.dtype), v)

    out = lax.map(block, jnp.arange(S // Q_BLOCK))
    return out.transpose(1, 0, 2, 3, 4).reshape(B, S, H, D)


def dsa_attention(q, k, v, q_idx, k_idx, w_idx, topk):
    B, S, H, D = q.shape
    scale = D ** -0.5
    kpos = jnp.arange(S)
    gather = jax.vmap(lambda arr, idx: arr[idx])

    def block(i):
        t0 = i * Q_BLOCK
        qpos = t0 + jnp.arange(Q_BLOCK)
        qi = lax.dynamic_slice_in_dim(q_idx, t0, Q_BLOCK, axis=1)
        wi = lax.dynamic_slice_in_dim(w_idx, t0, Q_BLOCK, axis=1).astype(jnp.float32)
        qb = lax.dynamic_slice_in_dim(q, t0, Q_BLOCK, axis=1)
        logits = jnp.einsum('bqhd,bkd->bqhk', qi, k_idx, preferred_element_type=jnp.float32)
        score = jnp.einsum('bqh,bqhk->bqk', wi, jax.nn.relu(logits))
        visible = kpos[None, :] <= qpos[:, None]
        score = jnp.where(visible[None], score, -jnp.inf)
        _, sel = lax.top_k(score, topk)
        valid = sel <= qpos[None, :, None]
        ks = gather(k, sel)
        vs = gather(v, sel)
        s = jnp.einsum('bqhd,bqkhd->bhqk', qb, ks, preferred_element_type=jnp.float32) * scale
        s = jnp.where(valid[:, None], s, -jnp.inf)
        prob = jax.nn.softmax(s, axis=-1)
        return jnp.einsum('bhqk,bqkhd->bqhd', prob.astype(v.dtype), vs)

    out = lax.map(block, jnp.arange(S // Q_BLOCK))
    return out.transpose(1, 0, 2, 3, 4).reshape(B, S, H, D)


def dilated_window(q, k, v, window, dil):
    B, S, H, D = q.shape
    span = window // dil
    unit = dil * Q_BLOCK
    s_pad = -(-S // unit) * unit
    n_chunk = s_pad // unit
    scale = D ** -0.5

    def to_sub(x):
        x = jnp.pad(x, ((0, 0), (0, s_pad - S), (0, 0), (0, 0)))
        x = x.reshape(B, s_pad // dil, dil, H, D).transpose(0, 2, 1, 3, 4)
        return x.reshape(B, dil, n_chunk, Q_BLOCK, H, D)

    def with_prev(x):
        prev = jnp.pad(x, ((0, 0), (0, 0), (1, 0), (0, 0), (0, 0), (0, 0)))[:, :, :-1]
        return jnp.concatenate([prev, x], axis=3)

    qs = to_sub(q)
    kk = with_prev(to_sub(k))
    vv = with_prev(to_sub(v))
    s = jnp.einsum('brcqhd,brckhd->brchqk', qs, kk, preferred_element_type=jnp.float32) * scale
    qi = jnp.arange(Q_BLOCK)[:, None]
    kj = jnp.arange(2 * Q_BLOCK)[None, :]
    dist = qi + Q_BLOCK - kj
    band = (dist >= 0) & (dist <= span)
    before_start = (jnp.arange(n_chunk) == 0)[:, None, None] & (kj < Q_BLOCK)[None]
    mask = band[None] & ~before_start
    s = jnp.where(mask[None, None, :, None], s, -jnp.inf)
    m = jnp.max(s, axis=-1, keepdims=True)
    e = jnp.exp(s - m)
    den = jnp.sum(e, axis=-1)
    o = jnp.einsum('brchqk,brckhd->brcqhd', e, vv.astype(jnp.float32)) / den.transpose(0, 1, 2, 4, 3)[..., None]
    lse = (m[..., 0] + jnp.log(den)).transpose(0, 1, 2, 4, 3)
    o = o.reshape(B, dil, s_pad // dil, H, D).transpose(0, 2, 1, 3, 4).reshape(B, s_pad, H, D)[:, :S]
    lse = lse.reshape(B, dil, s_pad // dil, H).transpose(0, 2, 1, 3).reshape(B, s_pad, H)[:, :S]
    return o, lse


def dilated_attention(q, k, v):
    outs, lses = [], []
    for window, dil in DIL_PATTERNS:
        o, lse = dilated_window(q, k, v, window, dil)
        outs.append(o)
        lses.append(lse)
    alpha = jax.nn.softmax(jnp.stack(lses, axis=0), axis=0)
    return jnp.einsum('pbsh,pbshd->bshd', alpha, jnp.stack(outs, axis=0)).astype(q.dtype)


def even_mixer(h, w_in, b_f, qn_a, kn_a, qn_b, kn_b, w_out):
    B, S, _ = h.shape
    pos = jnp.arange(S)
    topk = min(TOPK_MAX, S // 4)
    proj = h @ w_in
    cuts = [int(c) for c in np.cumsum(EVEN_SPLITS)[:-1]]
    qa, ka, va, fa, qb, kb, vb, qi, ki, wi = jnp.split(proj, cuts, axis=-1)
    heads = lambda t, n: t.reshape(B, S, n, HEAD_DIM)
    qa = rmsnorm(heads(qa, N_HEADS_A), qn_a)
    ka = rmsnorm(heads(ka, N_HEADS_A), kn_a)
    log_f = jax.nn.log_sigmoid(fa.astype(jnp.float32) + b_f.astype(jnp.float32))
    out_a = fox_attention(qa, ka, heads(va, N_HEADS_A), log_f)
    qb = rope(rmsnorm(heads(qb, N_HEADS_B), qn_b), pos)
    kb = rope(rmsnorm(heads(kb, N_HEADS_B), kn_b), pos)
    qi = rope(qi.reshape(B, S, IDX_HEADS, IDX_DIM), pos) * (IDX_DIM ** -0.5)
    ki = rope(ki[:, :, None, :], pos)[:, :, 0, :]
    wi = wi * (IDX_HEADS ** -0.5)
    out_b = dsa_attention(qb, kb, heads(vb, N_HEADS_B), qi, ki, wi, topk)
    merged = jnp.concatenate([out_a.reshape(B, S, -1), out_b.reshape(B, S, -1)], axis=-1)
    return merged @ w_out


def odd_mixer(h, w_in, qn_c, kn_c, w_out):
    B, S, _ = h.shape
    pos = jnp.arange(S)
    q, k, v = jnp.split(h @ w_in, 3, axis=-1)
    q = rope(rmsnorm(q.reshape(B, S, N_HEADS_C, HEAD_DIM), qn_c), pos)
    k = rope(rmsnorm(k.reshape(B, S, N_HEADS_C, HEAD_DIM), kn_c), pos)
    v = v.reshape(B, S, N_HEADS_C, HEAD_DIM)
    return dilated_attention(q, k, v).reshape(B, S, -1) @ w_out


def causal_dwconv(x, w, b):
    c = x.shape[-1]
    y = lax.conv_general_dilated(x, w[:, None, :].astype(x.dtype), window_strides=(1,),
                                 padding=[(CONV_WIDTH - 1, 0)],
                                 dimension_numbers=('NWC', 'WIO', 'NWC'),
                                 feature_group_count=c)
    return y + b


def conv_ffn(h, w_gate, w_up, conv_w, conv_b, w_down):
    g = causal_dwconv(h @ w_gate, conv_w, conv_b)
    return (jax.nn.gelu(g) * (h @ w_up)) @ w_down


def setup_inputs(seed: int = 0) -> dict:
    key = jax.random.key(seed)
    ks = jax.random.split(key, 24)
    f32 = jnp.float32
    nrm = lambda k, shape, sc: jax.random.normal(k, shape, f32) * sc
    gain = lambda k, shape: 1.0 + 0.05 * jax.random.normal(k, shape, f32)
    return {
        "x": nrm(ks[0], (BATCH, SEQ, D_MODEL), 1.0),
        "p": nrm(ks[1], (DEPTH, BATCH, SEQ, P_DIM), 1.0),
        "norm_mix": gain(ks[2], (DEPTH, D_MODEL)),
        "w_in_even": nrm(ks[3], (N_EVEN, D_MODEL, WIN_EVEN), D_MODEL ** -0.5),
        "b_forget": 3.0 + 0.5 * jax.random.normal(ks[4], (N_EVEN, N_HEADS_A), f32),
        "q_norm_a": gain(ks[5], (N_EVEN, HEAD_DIM)),
        "k_norm_a": gain(ks[6], (N_EVEN, HEAD_DIM)),
        "q_norm_b": gain(ks[7], (N_EVEN, HEAD_DIM)),
        "k_norm_b": gain(ks[8], (N_EVEN, HEAD_DIM)),
        "w_in_odd": nrm(ks[9], (N_ODD, D_MODEL, WIN_ODD), D_MODEL ** -0.5),
        "q_norm_c": gain(ks[10], (N_ODD, HEAD_DIM)),
        "k_norm_c": gain(ks[11], (N_ODD, HEAD_DIM)),
        "w_out": nrm(ks[12], (DEPTH, D_MODEL, D_MODEL), 0.5 * D_MODEL ** -0.5),
        "norm_ffn": gain(ks[13], (DEPTH, D_MODEL)),
        "w_gate": nrm(ks[14], (DEPTH, D_MODEL, D_FF), D_MODEL ** -0.5),
        "w_up": nrm(ks[15], (DEPTH, D_MODEL, D_FF), D_MODEL ** -0.5),
        "conv_w": nrm(ks[16], (DEPTH, CONV_WIDTH, D_FF), CONV_WIDTH ** -0.5),
        "conv_b": nrm(ks[17], (DEPTH, D_FF), 0.02),
        "w_down": nrm(ks[18], (DEPTH, D_FF, D_MODEL), 0.5 * D_FF ** -0.5),
        "norm_ple": gain(ks[19], (DEPTH, D_MODEL)),
        "w_ple_gate": nrm(ks[20], (DEPTH, D_MODEL, D_MODEL), D_MODEL ** -0.5),
        "w_ple_proj": nrm(ks[21], (DEPTH, P_DIM, D_MODEL), 0.5 * P_DIM ** -0.5),
    }


def reference(x, p, norm_mix, w_in_even, b_forget, q_norm_a, k_norm_a, q_norm_b, k_norm_b,
              w_in_odd, q_norm_c, k_norm_c, w_out, norm_ffn, w_gate, w_up, conv_w, conv_b,
              w_down, norm_ple, w_ple_gate, w_ple_proj):
    for i in range(DEPTH):
        h = rmsnorm(x, norm_mix[i])
        j = i // 2
        if i % 2 == 0:
            x = x + even_mixer(h, w_in_even[j], b_forget[j], q_norm_a[j], k_norm_a[j],
                               q_norm_b[j], k_norm_b[j], w_out[i])
        else:
            x = x + odd_mixer(h, w_in_odd[j], q_norm_c[j], k_norm_c[j], w_out[i])
        h = rmsnorm(x, norm_ffn[i])
        x = x + conv_ffn(h, w_gate[i], w_up[i], conv_w[i], conv_b[i], w_down[i])
        gate = jax.nn.sigmoid(rmsnorm(x, norm_ple[i]) @ w_ple_gate[i])
        x = x + gate * (p[i] @ w_ple_proj[i])
    return x
```

```python
import functools

import numpy as np
import jax
import jax.numpy as jnp
from jax import lax
from jax.experimental import pallas as pl
from jax.experimental.pallas import tpu as pltpu

F32 = jnp.float32
BF16 = jnp.bfloat16

D_MODEL = 1024
HEAD_DIM = 64
N_HEADS = D_MODEL // HEAD_DIM
N_HEADS_A = N_HEADS // 2
N_HEADS_B = N_HEADS - N_HEADS_A
IDX_HEADS = 16
IDX_DIM = 64
TOPK_MAX = 256
DIL_PATTERNS = ((128, 1), (512, 4), (2048, 16))
Q_BLOCK = 128
D_FF = 2816
P_DIM = 256
ROPE_THETA = 10000.0
EPS = 1e-6
DEPTH = 4

LANES = 128
LOG2E = 1.4426950408889634
NEG_INF = float("-inf")
INT_MIN = -(2 ** 31)
VMEM_LIMIT_BYTES = 56 * 1024 * 1024

EV_QA, EV_KA, EV_VA, EV_QB, EV_KB, EV_VB, EV_QI, EV_MISC = 0, 512, 1024, 1536, 2048, 2560, 3072, 4096
EV_WIDTH = 4224
MISC_F = 64
MISC_W = 72

_NT = (((1,), (1,)), ((), ()))


def _params(n_axes):
    return pltpu.CompilerParams(dimension_semantics=("arbitrary",) * n_axes,
                                vmem_limit_bytes=VMEM_LIMIT_BYTES)


def _rmsnorm_rows(x, g):
    ms = jnp.mean(x * x, axis=-1, keepdims=True)
    return x * lax.rsqrt(ms + EPS) * g


def _split3(x):
    hi = x.astype(BF16)
    r = x - hi.astype(F32)
    mid = r.astype(BF16)
    lo = (r - mid.astype(F32)).astype(BF16)
    return hi, mid, lo


def _rms_matmul_kernel(x_ref, g_ref, w_ref, o_ref):
    h = _rmsnorm_rows(x_ref[...], g_ref[...]).astype(BF16)
    o_ref[...] = jnp.dot(h, w_ref[...], preferred_element_type=F32)


def _rms_matmul(x, g, w, *, tm, tn):
    s, d = x.shape
    n = w.shape[1]
    return pl.pallas_call(
        _rms_matmul_kernel,
        grid=(n // tn, s // tm),
        in_specs=[pl.BlockSpec((tm, d), lambda j, i: (i, 0)),
                  pl.BlockSpec((1, d), lambda j, i: (0, 0)),
                  pl.BlockSpec((d, tn), lambda j, i: (0, j))],
        out_specs=pl.BlockSpec((tm, tn), lambda j, i: (i, j)),
        out_shape=jax.ShapeDtypeStruct((s, n), F32),
        compiler_params=_params(2),
        name="rms_matmul",
    )(x, g, w)


def _head_rmsnorm(x, gain, bmat):
    sq = x * x
    hi = sq.astype(BF16)
    lo = (sq - hi.astype(F32)).astype(BF16)
    ms = (jnp.dot(hi, bmat, preferred_element_type=F32)
          + jnp.dot(lo, bmat, preferred_element_type=F32))
    return x * lax.rsqrt(ms + EPS) * gain


def _rope(x, cos, sin_signed, low_half):
    rot = jnp.where(low_half, pltpu.roll(x, LANES - 32, 1), pltpu.roll(x, 32, 1))
    return x * cos + rot * sin_signed


def _log_sigmoid(z):
    return jnp.minimum(z, 0.0) - jnp.log1p(jnp.exp(-jnp.abs(z)))


def _prep_even_kernel(proj_ref, cos_ref, sin_ref, gqa_ref, gka_ref, gqb_ref, gkb_ref, bf_ref,
                      bmat_ref, tri_ref,
                      qa_ref, ka_ref, va_ref, qb_ref, kb_ref, vb_ref, qi_ref, kia_ref, kib_ref,
                      wi_ref, carry):
    tm = proj_ref.shape[0]

    @pl.when(pl.program_id(0) == 0)
    def _():
        carry[...] = jnp.zeros_like(carry)

    cos = cos_ref[...]
    sin = sin_ref[...]
    bmat = bmat_ref[...]
    lane = lax.broadcasted_iota(jnp.int32, (tm, LANES), 1)
    low_half = (lane & 32) == 0
    head0 = lane < HEAD_DIM

    misc = proj_ref[:, EV_MISC:EV_MISC + LANES]

    logf = _log_sigmoid(misc + bf_ref[...])
    logf = jnp.where((lane >= MISC_F) & (lane < MISC_F + N_HEADS_A), logf, 0.0)
    tri = tri_ref[...]
    f_hi, f_mid, f_lo = _split3(logf)
    c = (jnp.dot(tri, f_hi, preferred_element_type=F32)
         + jnp.dot(tri, f_mid, preferred_element_type=F32)
         + jnp.dot(tri, f_lo, preferred_element_type=F32)) + carry[0:1, :]
    carry[0:1, :] = c[tm - 1:tm, :]
    c_hi, c_mid, c_lo = (t.astype(F32) for t in _split3(c * LOG2E))

    def place(block, h):
        return block if h % 2 == 0 else pltpu.roll(block, HEAD_DIM, 1)

    for hp in range(N_HEADS_A // 2):
        cols = slice(hp * LANES, (hp + 1) * LANES)
        qa = _head_rmsnorm(proj_ref[:, EV_QA + hp * LANES:EV_QA + (hp + 1) * LANES], gqa_ref[...], bmat)
        ka = _head_rmsnorm(proj_ref[:, EV_KA + hp * LANES:EV_KA + (hp + 1) * LANES], gka_ref[...], bmat)
        va = proj_ref[:, EV_VA + hp * LANES:EV_VA + (hp + 1) * LANES]
        qb = _rope(_head_rmsnorm(proj_ref[:, EV_QB + hp * LANES:EV_QB + (hp + 1) * LANES], gqb_ref[...], bmat),
                   cos, sin, low_half)
        kb = _rope(_head_rmsnorm(proj_ref[:, EV_KB + hp * LANES:EV_KB + (hp + 1) * LANES], gkb_ref[...], bmat),
                   cos, sin, low_half)
        vb = proj_ref[:, EV_VB + hp * LANES:EV_VB + (hp + 1) * LANES]
        del cols
        for sub in range(2):
            h = 2 * hp + sub
            src = MISC_F + h
            q_aug = jnp.where(
                head0, place(qa, h),
                jnp.where(lane == 64, pltpu.roll(c_hi, (64 - src) % LANES, 1),
                          jnp.where(lane == 65, pltpu.roll(c_mid, (65 - src) % LANES, 1),
                                    jnp.where(lane == 66, pltpu.roll(c_lo, (66 - src) % LANES, 1),
                                              jnp.where(lane < 70, 1.0, 0.0)))))
            k_aug = jnp.where(
                head0, place(ka, h),
                jnp.where(lane < 67, 1.0,
                          jnp.where(lane == 67, -pltpu.roll(c_hi, (67 - src) % LANES, 1),
                                    jnp.where(lane == 68, -pltpu.roll(c_mid, (68 - src) % LANES, 1),
                                              jnp.where(lane == 69, -pltpu.roll(c_lo, (69 - src) % LANES, 1),
                                                        0.0)))))
            ones_col = jnp.where(lane == HEAD_DIM, 1.0, 0.0)
            qa_ref[h] = q_aug.astype(BF16)
            ka_ref[h] = k_aug.astype(BF16)
            va_ref[h] = jnp.where(head0, place(va, h), ones_col).astype(BF16)
            qb_ref[h] = jnp.where(head0, place(qb, h), 0.0).astype(BF16)
            kb_ref[h] = jnp.where(head0, place(kb, h), 0.0).astype(BF16)
            vb_ref[h] = jnp.where(head0, place(vb, h), ones_col).astype(BF16)

    for blk in range(IDX_HEADS * IDX_DIM // LANES):
        cols = slice(EV_QI + blk * LANES, EV_QI + (blk + 1) * LANES)
        qi = _rope(proj_ref[:, cols], cos, sin, low_half) * (IDX_DIM ** -0.5)
        qi_ref[:, blk * LANES:(blk + 1) * LANES] = qi.astype(BF16)
    ki = jnp.where(head0, _rope(misc, cos, sin, low_half), 0.0)
    kia_ref[...] = ki.astype(BF16)
    kib_ref[...] = pltpu.roll(ki, HEAD_DIM, 1).astype(BF16)
    wi_ref[...] = misc * (IDX_HEADS ** -0.5)


def _prep_even(proj, cos, sin, gqa, gka, gqb, gkb, bf_pad, bmat, tri, *, tm):
    s = proj.shape[0]
    row = lambda w: pl.BlockSpec((tm, w), lambda i: (i, 0))
    const = lambda a: pl.BlockSpec(a.shape, lambda i: (0, 0))
    heads = pl.BlockSpec((N_HEADS_A, tm, LANES), lambda i: (0, i, 0))
    heads_shape = jax.ShapeDtypeStruct((N_HEADS_A, s, LANES), BF16)
    return pl.pallas_call(
        _prep_even_kernel,
        grid=(s // tm,),
        in_specs=[row(EV_WIDTH), row(LANES), row(LANES), const(gqa), const(gka), const(gqb), const(gkb),
                  const(bf_pad), const(bmat), const(tri)],
        out_specs=[heads] * 6 + [row(IDX_HEADS * IDX_DIM), row(LANES), row(LANES), row(LANES)],
        out_shape=[heads_shape] * 6 + [jax.ShapeDtypeStruct((s, IDX_HEADS * IDX_DIM), BF16),
                                       jax.ShapeDtypeStruct((s, LANES), BF16),
                                       jax.ShapeDtypeStruct((s, LANES), BF16),
                                       jax.ShapeDtypeStruct((s, LANES), F32)],
        scratch_shapes=[pltpu.VMEM((8, LANES), F32)],
        compiler_params=_params(1),
        name="prep_even",
    )(proj, cos, sin, gqa, gka, gqb, gkb, bf_pad, bmat, tri)


def _indexer_kernel(qi_ref, kia_ref, kib_ref, wi_ref, bias_ref, key_scr, *, tq, tk, topk):
    i = pl.program_id(0)
    n_chunks = bias_ref.shape[0]
    nk = ((i + 1) * tq + tk - 1) // tk
    row = lax.broadcasted_iota(jnp.int32, (tq, tk), 0) + i * tq
    col = lax.broadcasted_iota(jnp.int32, (tq, tk), 1)
    w = wi_ref[...]

    def score_chunk(c, carry):
        ks = pl.multiple_of(c * tk, tk)
        ka = kia_ref[pl.ds(ks, tk), :]
        kb = kib_ref[pl.ds(ks, tk), :]
        acc = jnp.zeros((tq, tk), F32)
        for hp in range(IDX_HEADS // 2):
            q2 = qi_ref[:, hp * LANES:(hp + 1) * LANES]
            l0 = lax.dot_general(q2, ka, _NT, preferred_element_type=F32)
            l1 = lax.dot_general(q2, kb, _NT, preferred_element_type=F32)
            w0 = w[:, MISC_W + 2 * hp:MISC_W + 2 * hp + 1]
            w1 = w[:, MISC_W + 2 * hp + 1:MISC_W + 2 * hp + 2]
            acc = acc + w0 * jnp.maximum(l0, 0.0) + w1 * jnp.maximum(l1, 0.0)
        bits = pltpu.bitcast(acc, jnp.int32)
        key = bits ^ ((bits >> 31) & 0x7FFFFFFF)
        key_scr[c] = jnp.where(col + c * tk <= row, key, INT_MIN)
        return carry

    lax.fori_loop(0, nk, score_chunk, 0)

    def bit_body(it, t_u):
        bit = lax.shift_left(jnp.int32(1), 31 - it)
        cand_u = t_u | bit
        cand = cand_u ^ INT_MIN

        def count_chunk(c, cnt):
            ones = jnp.where(key_scr[c] >= cand, 1.0, 0.0)
            for t in range(tk // LANES):
                cnt = cnt + ones[:, t * LANES:(t + 1) * LANES]
            return cnt

        cnt = lax.fori_loop(0, nk, count_chunk, jnp.zeros((tq, LANES), F32))
        total = jnp.sum(cnt, axis=-1, keepdims=True)
        return jnp.where(total >= float(topk), cand_u, t_u)

    t_u = lax.fori_loop(0, 32, bit_body, jnp.zeros((tq, 1), jnp.int32))
    qpos = lax.broadcasted_iota(jnp.int32, (tq, 1), 0) + i * tq
    thr = jnp.where(qpos < topk, INT_MIN + 1, t_u ^ INT_MIN)

    def bias_chunk(c, carry):
        bias_ref[c] = jnp.where(key_scr[c] >= thr, 0.0, NEG_INF).astype(BF16)
        return carry

    lax.fori_loop(0, nk, bias_chunk, 0)

    def fill_chunk(c, carry):
        bias_ref[c] = jnp.full((tq, tk), NEG_INF, BF16)
        return carry

    lax.fori_loop(nk, n_chunks, fill_chunk, 0)


def _indexer(qi, kia, kib, wi, *, tq, tk, topk):
    s = qi.shape[0]
    n_chunks = s // tk
    return pl.pallas_call(
        functools.partial(_indexer_kernel, tq=tq, tk=tk, topk=topk),
        grid=(s // tq,),
        in_specs=[pl.BlockSpec((tq, qi.shape[1]), lambda i: (i, 0)),
                  pl.BlockSpec((s, LANES), lambda i: (0, 0)),
                  pl.BlockSpec((s, LANES), lambda i: (0, 0)),
                  pl.BlockSpec((tq, LANES), lambda i: (i, 0))],
        out_specs=pl.BlockSpec((n_chunks, tq, tk), lambda i: (0, i, 0)),
        out_shape=jax.ShapeDtypeStruct((n_chunks, s, tk), BF16),
        scratch_shapes=[pltpu.VMEM((n_chunks, tq, tk), jnp.int32)],
        compiler_params=_params(1),
        name="dsa_indexer",
    )(qi, kia, kib, wi)


def _attn_kernel(*refs, has_bias):
    if has_bias:
        q_ref, k_ref, v_ref, bias_ref, o_ref, m_scr, acc_scr = refs
    else:
        q_ref, k_ref, v_ref, o_ref, m_scr, acc_scr = refs
        bias_ref = None
    n_heads, tq, _ = q_ref.shape
    tk = k_ref.shape[1]
    i = pl.program_id(0)
    j = pl.program_id(1)

    @pl.when(j == 0)
    def _():
        m_scr[...] = jnp.full(m_scr.shape, NEG_INF, F32)
        acc_scr[...] = jnp.zeros(acc_scr.shape, F32)

    def step(diagonal):
        bias = bias_ref[...].astype(F32) if has_bias else None
        if diagonal:
            causal = (lax.broadcasted_iota(jnp.int32, (tq, tk), 1)
                      <= lax.broadcasted_iota(jnp.int32, (tq, tk), 0))

        def head(h, carry):
            s = lax.dot_general(q_ref[h], k_ref[h], _NT, preferred_element_type=F32)
            if has_bias:
                s = s + bias
            if diagonal:
                s = jnp.where(causal, s, NEG_INF)
            m_old = m_scr[h][:, 0:1]
            m_new = jnp.maximum(m_old, jnp.max(s, axis=-1, keepdims=True))
            m_safe = jnp.where(m_new == NEG_INF, 0.0, m_new)
            p = jnp.exp2(s - m_safe)
            alpha = jnp.exp2(m_old - m_safe)
            acc_scr[h] = alpha * acc_scr[h] + jnp.dot(p.astype(BF16), v_ref[h], preferred_element_type=F32)
            m_scr[h] = jnp.broadcast_to(m_new, (tq, LANES))
            return carry

        lax.fori_loop(0, n_heads, head, 0)

    @pl.when(j < i)
    def _():
        step(False)

    @pl.when(j == i)
    def _():
        step(True)
        for h in range(n_heads):
            a = acc_scr[h]
            o_ref[:, h * HEAD_DIM:(h + 1) * HEAD_DIM] = (
                a[:, 0:HEAD_DIM] / a[:, HEAD_DIM:HEAD_DIM + 1]).astype(o_ref.dtype)


def _attention(q, k, v, bias, *, t):
    n_heads, s, _ = q.shape
    kv_spec = pl.BlockSpec((n_heads, t, LANES), lambda i, j: (0, jnp.minimum(j, i), 0))
    in_specs = [pl.BlockSpec((n_heads, t, LANES), lambda i, j: (0, i, 0)), kv_spec, kv_spec]
    args = [q, k, v]
    if bias is not None:
        in_specs.append(pl.BlockSpec((None, t, t), lambda i, j: (jnp.minimum(j, i), i, 0)))
        args.append(bias)
    return pl.pallas_call(
        functools.partial(_attn_kernel, has_bias=bias is not None),
        grid=(s // t, s // t),
        in_specs=in_specs,
        out_specs=pl.BlockSpec((t, n_heads * HEAD_DIM), lambda i, j: (i, 0)),
        out_shape=jax.ShapeDtypeStruct((s, n_heads * HEAD_DIM), BF16),
        scratch_shapes=[pltpu.VMEM((n_heads, t, LANES), F32), pltpu.VMEM((n_heads, t, LANES), F32)],
        compiler_params=_params(2),
        name="dsa_attention" if bias is not None else "fox_attention",
    )(*args)


def _matmul_res_kernel(*refs, n_pairs):
    res_ref = refs[2 * n_pairs]
    o_ref = refs[2 * n_pairs + 1]
    acc = res_ref[...]
    for p in range(n_pairs):
        acc = acc + jnp.dot(refs[2 * p][...], refs[2 * p + 1][...], preferred_element_type=F32)
    o_ref[...] = acc


def _matmul_res(pairs, res, *, tm):
    s, n = res.shape
    in_specs, args = [], []
    for a, w in pairs:
        in_specs += [pl.BlockSpec((tm, a.shape[1]), lambda i: (i, 0)),
                     pl.BlockSpec(w.shape, lambda i: (0, 0))]
        args += [a, w]
    in_specs.append(pl.BlockSpec((tm, n), lambda i: (i, 0)))
    args.append(res)
    return pl.pallas_call(
        functools.partial(_matmul_res_kernel, n_pairs=len(pairs)),
        grid=(s // tm,),
        in_specs=in_specs,
        out_specs=pl.BlockSpec((tm, n), lambda i: (i, 0)),
        out_shape=jax.ShapeDtypeStruct((s, n), F32),
        compiler_params=_params(1),
        name="matmul_residual",
    )(*args)


def _prep_odd_kernel(proj_ref, cos_ref, sin_ref, gq_ref, gk_ref, bmat_ref, q_ref, k_ref, v_ref):
    tm = proj_ref.shape[0]
    cos = cos_ref[...]
    sin = sin_ref[...]
    bmat = bmat_ref[...]
    lane = lax.broadcasted_iota(jnp.int32, (tm, LANES), 1)
    low_half = (lane & 32) == 0
    for blk in range(D_MODEL // LANES):
        cols = slice(blk * LANES, (blk + 1) * LANES)
        q = proj_ref[:, blk * LANES:(blk + 1) * LANES]
        k = proj_ref[:, D_MODEL + blk * LANES:D_MODEL + (blk + 1) * LANES]
        q_ref[:, cols] = _rope(_head_rmsnorm(q, gq_ref[...], bmat), cos, sin, low_half).astype(BF16)
        k_ref[:, cols] = _rope(_head_rmsnorm(k, gk_ref[...], bmat), cos, sin, low_half).astype(BF16)
        v_ref[:, cols] = proj_ref[:, 2 * D_MODEL + blk * LANES:2 * D_MODEL + (blk + 1) * LANES].astype(BF16)


def _prep_odd(proj, cos, sin, gq, gk, bmat, *, tm):
    s = proj.shape[0]
    row = lambda w: pl.BlockSpec((tm, w), lambda i: (i, 0))
    const = lambda a: pl.BlockSpec(a.shape, lambda i: (0, 0))
    shape = jax.ShapeDtypeStruct((s, D_MODEL), BF16)
    return pl.pallas_call(
        _prep_odd_kernel,
        grid=(s // tm,),
        in_specs=[row(3 * D_MODEL), row(LANES), row(LANES), const(gq), const(gk), const(bmat)],
        out_specs=[row(D_MODEL)] * 3,
        out_shape=[shape] * 3,
        compiler_params=_params(1),
        name="prep_odd",
    )(proj, cos, sin, gq, gk, bmat)


def _dilated_kernel(q_ref, kc_ref, kp_ref, vc_ref, vp_ref, o_ref, lse_ref):
    c = pl.program_id(1)
    t = q_ref.shape[0]
    qi = lax.broadcasted_iota(jnp.int32, (t, t), 0)
    kj = lax.broadcasted_iota(jnp.int32, (t, t), 1)
    cur_ok = kj <= qi
    prev_ok = (kj >= qi) & (c > 0)
    lane = lax.broadcasted_iota(jnp.int32, (t, LANES), 1)
    lse_blk = jnp.zeros((t, LANES), F32)
    for h in range(N_HEADS):
        cols = slice(h * HEAD_DIM, (h + 1) * HEAD_DIM)
        q = q_ref[:, cols]
        s_cur = jnp.where(cur_ok, lax.dot_general(q, kc_ref[:, cols], _NT, preferred_element_type=F32), NEG_INF)
        s_prev = jnp.where(prev_ok, lax.dot_general(q, kp_ref[:, cols], _NT, preferred_element_type=F32), NEG_INF)
        m = jnp.maximum(jnp.max(s_cur, axis=-1, keepdims=True), jnp.max(s_prev, axis=-1, keepdims=True))
        e_cur = jnp.exp2(s_cur - m)
        e_prev = jnp.exp2(s_prev - m)
        den = jnp.sum(e_cur, axis=-1, keepdims=True) + jnp.sum(e_prev, axis=-1, keepdims=True)
        o = (jnp.dot(e_cur.astype(BF16), vc_ref[:, cols], preferred_element_type=F32)
             + jnp.dot(e_prev.astype(BF16), vp_ref[:, cols], preferred_element_type=F32))
        o_ref[:, cols] = o / den
        lse_blk = jnp.where(lane == h, m + jnp.log2(den), lse_blk)
    lse_ref[...] = lse_blk


def _dilated(q, k, v, dil):
    s = q.shape[0]
    rows = s // dil
    q2, k2, v2 = (a.reshape(rows, dil * D_MODEL) for a in (q, k, v))
    cur = pl.BlockSpec((Q_BLOCK, D_MODEL), lambda r, c: (c, r))
    prev = pl.BlockSpec((Q_BLOCK, D_MODEL), lambda r, c: (jnp.maximum(c - 1, 0), r))
    o, lse = pl.pallas_call(
        _dilated_kernel,
        grid=(dil, rows // Q_BLOCK),
        in_specs=[cur, cur, prev, cur, prev],
        out_specs=[cur, pl.BlockSpec((Q_BLOCK, LANES), lambda r, c: (c, r))],
        out_shape=[jax.ShapeDtypeStruct((rows, dil * D_MODEL), F32),
                   jax.ShapeDtypeStruct((rows, dil * LANES), F32)],
        compiler_params=_params(2),
        name=f"dilated_d{dil}",
    )(q2, k2, k2, v2, v2)
    return o.reshape(s, D_MODEL), lse.reshape(s, LANES)


def _merge_out_kernel(o1_ref, o2_ref, o3_ref, l1_ref, l2_ref, l3_ref, e_ref, w_ref, res_ref, out_ref):
    l1, l2, l3 = l1_ref[...], l2_ref[...], l3_ref[...]
    m = jnp.maximum(jnp.maximum(l1, l2), l3)
    w1, w2, w3 = jnp.exp2(l1 - m), jnp.exp2(l2 - m), jnp.exp2(l3 - m)
    inv = 1.0 / (w1 + w2 + w3)
    e = e_ref[...]

    def expand(a):
        hi = a.astype(BF16)
        lo = (a - hi.astype(F32)).astype(BF16)
        return jnp.dot(hi, e, preferred_element_type=F32) + jnp.dot(lo, e, preferred_element_type=F32)

    merged = (expand(w1 * inv) * o1_ref[...] + expand(w2 * inv) * o2_ref[...]
              + expand(w3 * inv) * o3_ref[...])
    out_ref[...] = res_ref[...] + jnp.dot(merged.astype(BF16), w_ref[...], preferred_element_type=F32)


def _merge_out(os_, lses, expand, w, res, *, tm):
    s = res.shape[0]
    row = lambda width: pl.BlockSpec((tm, width), lambda i: (i, 0))
    const = lambda a: pl.BlockSpec(a.shape, lambda i: (0, 0))
    return pl.pallas_call(
        _merge_out_kernel,
        grid=(s // tm,),
        in_specs=[row(D_MODEL)] * 3 + [row(LANES)] * 3 + [const(expand), const(w), row(D_MODEL)],
        out_specs=row(D_MODEL),
        out_shape=jax.ShapeDtypeStruct((s, D_MODEL), F32),
        compiler_params=_params(1),
        name="dilated_merge_out",
    )(*os_, *lses, expand, w, res)


def _gelu_tanh(x):
    return 0.5 * x * (1.0 + jnp.tanh(np.sqrt(2.0 / np.pi).astype(np.float32) * (x + 0.044715 * (x * x * x))))


def _ffn_act_kernel(x_ref, halo_ref, g_ref, wg_ref, wu_ref, cw_ref, cb_ref, o_ref):
    i = pl.program_id(1)
    tm = x_ref.shape[0]
    h = _rmsnorm_rows(x_ref[...], g_ref[...]).astype(BF16)
    hh = _rmsnorm_rows(halo_ref[...], g_ref[...]).astype(BF16)
    wg = wg_ref[...]
    gate = jnp.dot(h, wg, preferred_element_type=F32)
    gate_halo = jnp.where(i > 0, jnp.dot(hh, wg, preferred_element_type=F32), 0.0)
    up = jnp.dot(h, wu_ref[...], preferred_element_type=F32)
    row = lax.broadcasted_iota(jnp.int32, gate.shape, 0)
    g_m1 = jnp.where(row == 0, gate_halo[7:8, :], pltpu.roll(gate, 1, 0))
    g_m2 = jnp.where(row == 0, gate_halo[6:7, :],
                     jnp.where(row == 1, gate_halo[7:8, :], pltpu.roll(gate, 2, 0)))
    cw = cw_ref[...]
    conv = cw[2:3, :] * gate + cw[1:2, :] * g_m1 + cw[0:1, :] * g_m2 + cb_ref[...]
    o_ref[...] = (_gelu_tanh(conv) * up).astype(o_ref.dtype)
    del tm


def _ffn_act(x, g, wg, wu, cw, cb, *, tm, tn):
    s, d = x.shape
    n = wg.shape[1]
    halo_rows = 8
    return pl.pallas_call(
        _ffn_act_kernel,
        grid=(n // tn, s // tm),
        in_specs=[pl.BlockSpec((tm, d), lambda j, i: (i, 0)),
                  pl.BlockSpec((halo_rows, d), lambda j, i: (jnp.maximum(i * (tm // halo_rows) - 1, 0), 0)),
                  pl.BlockSpec((1, d), lambda j, i: (0, 0)),
                  pl.BlockSpec((d, tn), lambda j, i: (0, j)),
                  pl.BlockSpec((d, tn), lambda j, i: (0, j)),
                  pl.BlockSpec((cw.shape[0], tn), lambda j, i: (0, j)),
                  pl.BlockSpec((1, tn), lambda j, i: (0, j))],
        out_specs=pl.BlockSpec((tm, tn), lambda j, i: (i, j)),
        out_shape=jax.ShapeDtypeStruct((s, n), BF16),
        compiler_params=_params(2),
        name="ffn_act",
    )(x, x, g, wg, wu, cw, cb)


def _ple_kernel(x_ref, p_ref, g_ref, wg_ref, wp_ref, o_ref):
    x = x_ref[...]
    z = jnp.dot(_rmsnorm_rows(x, g_ref[...]).astype(BF16), wg_ref[...], preferred_element_type=F32)
    gate = 1.0 / (1.0 + jnp.exp(-z))
    proj = jnp.dot(p_ref[...].astype(BF16), wp_ref[...], preferred_element_type=F32)
    o_ref[...] = x + gate * proj


def _ple(x, p, g, wg, wp, *, tm):
    s, d = x.shape
    return pl.pallas_call(
        _ple_kernel,
        grid=(s // tm,),
        in_specs=[pl.BlockSpec((tm, d), lambda i: (i, 0)),
                  pl.BlockSpec((tm, p.shape[1]), lambda i: (i, 0)),
                  pl.BlockSpec((1, d), lambda i: (0, 0)),
                  pl.BlockSpec(wg.shape, lambda i: (0, 0)),
                  pl.BlockSpec(wp.shape, lambda i: (0, 0))],
        out_specs=pl.BlockSpec((tm, d), lambda i: (i, 0)),
        out_shape=jax.ShapeDtypeStruct((s, d), F32),
        compiler_params=_params(1),
        name="ple",
    )(x, p, g, wg, wp)


def _rope_tables(s):
    half = HEAD_DIM // 2
    inv_freq = ROPE_THETA ** (-jnp.arange(half, dtype=F32) / half)
    ang = jnp.arange(s).astype(F32)[:, None] * inv_freq[None, :]
    cos, sin = jnp.cos(ang), jnp.sin(ang)
    return jnp.tile(cos, (1, 4)), jnp.tile(jnp.concatenate([-sin, sin], axis=1), (1, 2))


def _gain_tile(g, scale=1.0):
    return (jnp.tile(g.astype(F32), 2) * scale)[None, :]


def _even_w_in(w):
    hd = N_HEADS_A * HEAD_DIM
    cuts = np.cumsum([hd, hd, hd, N_HEADS_A, hd, hd, hd, IDX_HEADS * IDX_DIM, IDX_DIM, IDX_HEADS])[:-1]
    qa, ka, va, fa, qb, kb, vb, qi, ki, wi = jnp.split(w, [int(c) for c in cuts], axis=1)
    pad = jnp.zeros((w.shape[0], LANES - IDX_DIM - N_HEADS_A - IDX_HEADS), w.dtype)
    return jnp.concatenate([qa, ka, va, qb, kb, vb, qi, ki, fa, wi, pad], axis=1).astype(BF16)


def kernel(x, p, norm_mix, w_in_even, b_forget, q_norm_a, k_norm_a, q_norm_b, k_norm_b, w_in_odd, q_norm_c,
           k_norm_c, w_out, norm_ffn, w_gate, w_up, conv_w, conv_b, w_down, norm_ple, w_ple_gate, w_ple_proj):
    batch, s, d = x.shape
    assert batch == 1 and d == D_MODEL
    depth = norm_mix.shape[0]
    topk = min(TOPK_MAX, s // 4)
    t_attn = min(512, s)
    tm = min(512, s)
    tm_prep = min(256, s)
    qscale = HEAD_DIM ** -0.5 * LOG2E

    cos, sin = _rope_tables(s)
    lane = np.arange(LANES)
    bmat = jnp.asarray((lane[:, None] // HEAD_DIM == lane[None, :] // HEAD_DIM) / HEAD_DIM, BF16)
    tri = jnp.asarray(np.tril(np.ones((tm_prep, tm_prep))), BF16)
    expand = jnp.asarray(lane[:, None] == np.arange(D_MODEL)[None, :] // HEAD_DIM, BF16)

    xs = x[0]
    for i in range(depth):
        j = i // 2
        g_mix = norm_mix[i][None, :]
        w_o = w_out[i].astype(BF16)
        if i % 2 == 0:
            proj = _rms_matmul(xs, g_mix, _even_w_in(w_in_even[j]), tm=tm, tn=EV_WIDTH // 3)
            bf_pad = jnp.zeros((1, LANES), F32).at[0, MISC_F:MISC_F + N_HEADS_A].set(b_forget[j])
            qa, ka, va, qb, kb, vb, qi, kia, kib, wi = _prep_even(
                proj, cos, sin, _gain_tile(q_norm_a[j], qscale), _gain_tile(k_norm_a[j]),
                _gain_tile(q_norm_b[j], qscale), _gain_tile(k_norm_b[j]), bf_pad, bmat, tri, tm=tm_prep)
            out_a = _attention(qa, ka, va, None, t=t_attn)
            bias = _indexer(qi, kia, kib, wi, tq=min(128, s), tk=t_attn, topk=topk)
            out_b = _attention(qb, kb, vb, bias, t=t_attn)
            half = N_HEADS_A * HEAD_DIM
            xs = _matmul_res([(out_a, w_o[:half]), (out_b, w_o[half:])], xs, tm=tm)
        else:
            proj = _rms_matmul(xs, g_mix, w_in_odd[j].astype(BF16), tm=tm, tn=D_MODEL)
            q, k, v = _prep_odd(proj, cos, sin, _gain_tile(q_norm_c[j], qscale), _gain_tile(k_norm_c[j]),
                                bmat, tm=tm_prep)
            outs, lses = zip(*[_dilated(q, k, v, dil) for _, dil in DIL_PATTERNS])
            xs = _merge_out(outs, lses, expand, w_o, xs, tm=tm_prep)
        act = _ffn_act(xs, norm_ffn[i][None, :], w_gate[i].astype(BF16), w_up[i].astype(BF16),
                       conv_w[i], conv_b[i][None, :], tm=tm_prep, tn=D_FF // 2)
        xs = _matmul_res([(act, w_down[i].astype(BF16))], xs, tm=tm)
        xs = _ple(xs, p[i, 0], norm_ple[i][None, :], w_ple_gate[i].astype(BF16),
                  w_ple_proj[i].astype(BF16), tm=tm)
    return xs[None]
```

```python
import functools

import numpy as np
import jax
import jax.numpy as jnp
from jax import lax
from jax.experimental import pallas as pl
from jax.experimental.pallas import tpu as pltpu

F32 = jnp.float32
BF16 = jnp.bfloat16

D_MODEL = 1024
HEAD_DIM = 64
N_HEADS = D_MODEL // HEAD_DIM
N_HEADS_A = N_HEADS // 2
N_HEADS_B = N_HEADS - N_HEADS_A
IDX_HEADS = 16
IDX_DIM = 64
TOPK_MAX = 256
DIL_PATTERNS = ((128, 1), (512, 4), (2048, 16))
Q_BLOCK = 128
D_FF = 2816
P_DIM = 256
ROPE_THETA = 10000.0
EPS = 1e-6
DEPTH = 4

LANES = 128
LOG2E = 1.4426950408889634
NEG_INF = float("-inf")
INT_MIN = -(2 ** 31)
VMEM_LIMIT_BYTES = 56 * 1024 * 1024

EV_QA, EV_KA, EV_VA, EV_QB, EV_KB, EV_VB, EV_QI, EV_MISC = 0, 512, 1024, 1536, 2048, 2560, 3072, 4096
EV_WIDTH = 4224
MISC_F = 64
MISC_W = 72

_NT = (((1,), (1,)), ((), ()))


def _params(n_axes):
    return pltpu.CompilerParams(dimension_semantics=("arbitrary",) * n_axes,
                                vmem_limit_bytes=VMEM_LIMIT_BYTES)


def _rmsnorm_rows(x, g):
    ms = jnp.mean(x * x, axis=-1, keepdims=True)
    return x * lax.rsqrt(ms + EPS) * g


def _split3(x):
    hi = x.astype(BF16)
    r = x - hi.astype(F32)
    mid = r.astype(BF16)
    lo = (r - mid.astype(F32)).astype(BF16)
    return hi, mid, lo


def _rms_matmul_kernel(x_ref, g_ref, w_ref, o_ref):
    h = _rmsnorm_rows(x_ref[...], g_ref[...]).astype(BF16)
    o_ref[...] = jnp.dot(h, w_ref[...], preferred_element_type=F32)


def _rms_matmul(x, g, w, *, tm, tn):
    s, d = x.shape
    n = w.shape[1]
    return pl.pallas_call(
        _rms_matmul_kernel,
        grid=(n // tn, s // tm),
        in_specs=[pl.BlockSpec((tm, d), lambda j, i: (i, 0)),
                  pl.BlockSpec((1, d), lambda j, i: (0, 0)),
                  pl.BlockSpec((d, tn), lambda j, i: (0, j))],
        out_specs=pl.BlockSpec((tm, tn), lambda j, i: (i, j)),
        out_shape=jax.ShapeDtypeStruct((s, n), F32),
        compiler_params=_params(2),
        name="rms_matmul",
    )(x, g, w)


def _head_rmsnorm(x, gain, bmat):
    sq = x * x
    hi = sq.astype(BF16)
    lo = (sq - hi.astype(F32)).astype(BF16)
    ms = (jnp.dot(hi, bmat, preferred_element_type=F32)
          + jnp.dot(lo, bmat, preferred_element_type=F32))
    return x * lax.rsqrt(ms + EPS) * gain


def _rope(x, cos, sin_signed, low_half):
    rot = jnp.where(low_half, pltpu.roll(x, LANES - 32, 1), pltpu.roll(x, 32, 1))
    return x * cos + rot * sin_signed


def _log_sigmoid(z):
    return jnp.minimum(z, 0.0) - jnp.log1p(jnp.exp(-jnp.abs(z)))


def _prep_even_kernel(proj_ref, cos_ref, sin_ref, gqa_ref, gka_ref, gqb_ref, gkb_ref, bf_ref,
                      bmat_ref, tri_ref,
                      qa_ref, ka_ref, va_ref, qb_ref, kb_ref, vb_ref, qi_ref, ki_ref, wi_ref, carry):
    tm = proj_ref.shape[0]

    @pl.when(pl.program_id(0) == 0)
    def _():
        carry[...] = jnp.zeros_like(carry)

    cos = cos_ref[...]
    sin = sin_ref[...]
    bmat = bmat_ref[...]
    lane = lax.broadcasted_iota(jnp.int32, (tm, LANES), 1)
    low_half = (lane & 32) == 0
    head0 = lane < HEAD_DIM

    misc = proj_ref[:, EV_MISC:EV_MISC + LANES]

    logf = _log_sigmoid(misc + bf_ref[...])
    logf = jnp.where((lane >= MISC_F) & (lane < MISC_F + N_HEADS_A), logf, 0.0)
    tri = tri_ref[...]
    f_hi, f_mid, f_lo = _split3(logf)
    c = (jnp.dot(tri, f_hi, preferred_element_type=F32)
         + jnp.dot(tri, f_mid, preferred_element_type=F32)
         + jnp.dot(tri, f_lo, preferred_element_type=F32)) + carry[0:1, :]
    carry[0:1, :] = c[tm - 1:tm, :]
    c_hi, c_mid, c_lo = (t.astype(F32) for t in _split3(c * LOG2E))

    def place(block, h):
        return block if h % 2 == 0 else pltpu.roll(block, HEAD_DIM, 1)

    for hp in range(N_HEADS_A // 2):
        cols = slice(hp * LANES, (hp + 1) * LANES)
        qa = _head_rmsnorm(proj_ref[:, EV_QA + hp * LANES:EV_QA + (hp + 1) * LANES], gqa_ref[...], bmat)
        ka = _head_rmsnorm(proj_ref[:, EV_KA + hp * LANES:EV_KA + (hp + 1) * LANES], gka_ref[...], bmat)
        va = proj_ref[:, EV_VA + hp * LANES:EV_VA + (hp + 1) * LANES]
        qb = _rope(_head_rmsnorm(proj_ref[:, EV_QB + hp * LANES:EV_QB + (hp + 1) * LANES], gqb_ref[...], bmat),
                   cos, sin, low_half)
        kb = _rope(_head_rmsnorm(proj_ref[:, EV_KB + hp * LANES:EV_KB + (hp + 1) * LANES], gkb_ref[...], bmat),
                   cos, sin, low_half)
        vb = proj_ref[:, EV_VB + hp * LANES:EV_VB + (hp + 1) * LANES]
        del cols
        for sub in range(2):
            h = 2 * hp + sub
            src = MISC_F + h
            q_aug = jnp.where(
                head0, place(qa, h),
                jnp.where(lane == 64, pltpu.roll(c_hi, (64 - src) % LANES, 1),
                          jnp.where(lane == 65, pltpu.roll(c_mid, (65 - src) % LANES, 1),
                                    jnp.where(lane == 66, pltpu.roll(c_lo, (66 - src) % LANES, 1),
                                              jnp.where(lane < 70, 1.0, 0.0)))))
            k_aug = jnp.where(
                head0, place(ka, h),
                jnp.where(lane < 67, 1.0,
                          jnp.where(lane == 67, -pltpu.roll(c_hi, (67 - src) % LANES, 1),
                                    jnp.where(lane == 68, -pltpu.roll(c_mid, (68 - src) % LANES, 1),
                                              jnp.where(lane == 69, -pltpu.roll(c_lo, (69 - src) % LANES, 1),
                                                        0.0)))))
            ones_col = jnp.where(lane == HEAD_DIM, 1.0, 0.0)
            qa_ref[h] = q_aug.astype(BF16)
            ka_ref[h] = k_aug.astype(BF16)
            va_ref[h] = jnp.where(head0, place(va, h), ones_col).T.astype(BF16)
            qb_ref[h] = jnp.where(head0, place(qb, h), 0.0).astype(BF16)
            kb_ref[h] = jnp.where(head0, place(kb, h), 0.0).astype(BF16)
            vb_ref[h] = jnp.where(head0, place(vb, h), ones_col).T.astype(BF16)

    for blk in range(IDX_HEADS * IDX_DIM // LANES):
        cols = slice(EV_QI + blk * LANES, EV_QI + (blk + 1) * LANES)
        qi = _rope(proj_ref[:, cols], cos, sin, low_half) * (IDX_DIM ** -0.5)
        qi_ref[2 * blk] = qi[:, 0:IDX_DIM].astype(BF16)
        qi_ref[2 * blk + 1] = qi[:, IDX_DIM:2 * IDX_DIM].astype(BF16)
    ki_ref[...] = _rope(misc, cos, sin, low_half)[:, 0:IDX_DIM].astype(BF16)
    wi_ref[...] = (misc * (IDX_HEADS ** -0.5)).T


def _prep_even(proj, cos, sin, gqa, gka, gqb, gkb, bf_pad, bmat, tri, *, tm):
    s = proj.shape[0]
    row = lambda w: pl.BlockSpec((tm, w), lambda i: (i, 0))
    const = lambda a: pl.BlockSpec(a.shape, lambda i: (0, 0))
    heads = pl.BlockSpec((N_HEADS_A, tm, LANES), lambda i: (0, i, 0))
    heads_shape = jax.ShapeDtypeStruct((N_HEADS_A, s, LANES), BF16)
    heads_t = pl.BlockSpec((N_HEADS_A, LANES, tm), lambda i: (0, 0, i))
    heads_t_shape = jax.ShapeDtypeStruct((N_HEADS_A, LANES, s), BF16)
    return pl.pallas_call(
        _prep_even_kernel,
        grid=(s // tm,),
        in_specs=[row(EV_WIDTH), row(LANES), row(LANES), const(gqa), const(gka), const(gqb), const(gkb),
                  const(bf_pad), const(bmat), const(tri)],
        out_specs=[heads, heads, heads_t, heads, heads, heads_t,
                   pl.BlockSpec((IDX_HEADS, tm, IDX_DIM), lambda i: (0, i, 0)),
                   row(IDX_DIM),
                   pl.BlockSpec((LANES, tm), lambda i: (0, i))],
        out_shape=[heads_shape, heads_shape, heads_t_shape, heads_shape, heads_shape, heads_t_shape,
                   jax.ShapeDtypeStruct((IDX_HEADS, s, IDX_DIM), BF16),
                   jax.ShapeDtypeStruct((s, IDX_DIM), BF16),
                   jax.ShapeDtypeStruct((LANES, s), F32)],
        scratch_shapes=[pltpu.VMEM((8, LANES), F32)],
        compiler_params=_params(1),
        name="prep_even",
    )(proj, cos, sin, gqa, gka, gqb, gkb, bf_pad, bmat, tri)


def _float_key(x):
    bits = pltpu.bitcast(x, jnp.int32)
    return bits ^ ((bits >> 31) & 0x7FFFFFFF)


def _key_float(k):
    return pltpu.bitcast(k ^ ((k >> 31) & 0x7FFFFFFF), F32)


def _kth_search(count_fn, lo_k, hi_k, c_lo, done, kf):
    def finished(lo, hi, c, d):
        return jnp.where(d | (c == kf) | (lo + 1 >= hi), 1.0, 0.0)

    def cond(state):
        return state[4] > 0.0

    def body(state):
        lo, hi, c, d, _ = state
        live = d == 0.0
        cand = (lo >> 1) + (hi >> 1) + (lo & hi & 1)
        n = count_fn(_key_float(cand))
        take = (n >= kf) & live
        drop = (n < kf) & live
        lo = jnp.where(take, cand, lo)
        c = jnp.where(take, n, c)
        hi = jnp.where(drop, cand, hi)
        d = finished(lo, hi, c, d > 0.0)
        return lo, hi, c, d, jnp.max(1.0 - d)

    d0 = finished(lo_k, hi_k, c_lo, done)
    lo_k, _, c_lo, _, _ = lax.while_loop(cond, body, (lo_k, hi_k, c_lo, d0, jnp.max(1.0 - d0)))
    return lo_k, c_lo


def _indexer_kernel(qi_ref, ki_ref, wi_ref, bias_ref, sc_scr, g_scr, *, tq, tk, topk, group):
    i = pl.program_id(0)
    s_keys = sc_scr.shape[0]
    n_chunks = s_keys // tk
    nk = ((i + 1) * tq + tk - 1) // tk
    sub = 8
    g_rows = tk // group
    kf = float(topk)
    lowest = float(np.finfo(np.float32).min)
    qpos = lax.broadcasted_iota(jnp.int32, (1, tq), 1) + i * tq
    w = wi_ref[...]

    g_scr[...] = jnp.full(g_scr.shape, NEG_INF, F32)

    def score_chunk(c, gmax):
        ks = pl.multiple_of(c * tk, tk)
        ki = ki_ref[pl.ds(ks, tk), :]
        acc = jnp.zeros((tk, tq), F32)
        for hp in range(IDX_HEADS // 2):
            q2 = jnp.concatenate([qi_ref[2 * hp], qi_ref[2 * hp + 1]], axis=0)
            l01 = lax.dot_general(ki, q2, _NT, preferred_element_type=F32)
            w0 = w[MISC_W + 2 * hp:MISC_W + 2 * hp + 1, :]
            w1 = w[MISC_W + 2 * hp + 1:MISC_W + 2 * hp + 2, :]
            acc = acc + w0 * jnp.maximum(l01[:, 0:tq], 0.0) + w1 * jnp.maximum(l01[:, tq:2 * tq], 0.0)
        kpos = lax.broadcasted_iota(jnp.int32, (tk, tq), 0) + c * tk
        sc = jnp.where(kpos <= qpos, acc, NEG_INF)
        sc_scr[pl.ds(ks, tk), :] = sc
        g = jnp.max(sc.reshape(tk // (group * sub), group, sub, tq), axis=1).reshape(g_rows, tq)
        g_scr[pl.ds(pl.multiple_of(c * g_rows, g_rows), g_rows), :] = g
        return jnp.maximum(gmax, jnp.max(g, axis=0, keepdims=True))

    gmax = lax.fori_loop(0, nk, score_chunk, jnp.full((1, tq), NEG_INF, F32))

    def count_rows(ref, rows_per_step, n_steps, cand):
        ways = min(8, rows_per_step // sub)

        def step(c, cnt):
            r0 = pl.multiple_of(c * rows_per_step, rows_per_step)
            ones = jnp.where(ref[pl.ds(r0, rows_per_step), :] >= cand, 1.0, 0.0)
            return cnt + jnp.sum(ones.reshape(rows_per_step // (ways * sub), ways, sub, tq), axis=0)

        cnt = lax.fori_loop(0, n_steps, step, jnp.zeros((ways, sub, tq), F32))
        return jnp.sum(jnp.sum(cnt, axis=0), axis=0, keepdims=True)

    count_groups = lambda cand: count_rows(g_scr, g_rows, nk, cand)
    count_all = lambda cand: count_rows(sc_scr, tk, nk, cand)

    lo0 = _float_key(jnp.full((1, tq), lowest, F32))
    hi0 = _float_key(gmax) + 1
    few = qpos < topk
    c_groups = count_groups(jnp.full((1, tq), lowest, F32))
    u_k, _ = _kth_search(count_groups, lo0, hi0, c_groups, few | (c_groups < kf), kf)
    c_u = count_all(_key_float(u_k))
    t_k, _ = _kth_search(count_all, u_k, hi0, c_u, few, kf)
    thr = jnp.where(few, lowest, _key_float(t_k))

    def bias_chunk(c, carry):
        ks = pl.multiple_of(c * tk, tk)
        bias_ref[pl.ds(ks, tk), :] = jnp.where(sc_scr[pl.ds(ks, tk), :] >= thr, 0.0, NEG_INF).astype(BF16)
        return carry

    lax.fori_loop(0, nk, bias_chunk, 0)

    def fill_chunk(c, carry):
        bias_ref[pl.ds(pl.multiple_of(c * tk, tk), tk), :] = jnp.full((tk, tq), NEG_INF, BF16)
        return carry

    lax.fori_loop(nk, n_chunks, fill_chunk, 0)


def _indexer(qi, ki, wi_t, *, tq, tk, topk, group):
    s = ki.shape[0]
    return pl.pallas_call(
        functools.partial(_indexer_kernel, tq=tq, tk=tk, topk=topk, group=group),
        grid=(s // tq,),
        in_specs=[pl.BlockSpec((IDX_HEADS, tq, IDX_DIM), lambda i: (0, i, 0)),
                  pl.BlockSpec((s, IDX_DIM), lambda i: (0, 0)),
                  pl.BlockSpec((LANES, tq), lambda i: (0, i))],
        out_specs=pl.BlockSpec((None, s, tq), lambda i: (i, 0, 0)),
        out_shape=jax.ShapeDtypeStruct((s // tq, s, tq), BF16),
        scratch_shapes=[pltpu.VMEM((s, tq), F32), pltpu.VMEM((s // group, tq), F32)],
        compiler_params=_params(1),
        name="dsa_indexer",
    )(qi, ki, wi_t)


def _attn_kernel(*refs, has_bias):
    if has_bias:
        q_ref, k_ref, vt_ref, bias_ref, o_ref, m_scr, acc_scr = refs
    else:
        q_ref, k_ref, vt_ref, o_ref, m_scr, acc_scr = refs
        bias_ref = None
    n_heads, t, _ = q_ref.shape
    sub = m_scr.shape[1]
    i = pl.program_id(0)
    j = pl.program_id(1)

    @pl.when(j == 0)
    def _():
        m_scr[...] = jnp.full(m_scr.shape, NEG_INF, F32)
        acc_scr[...] = jnp.zeros(acc_scr.shape, F32)

    def step(diagonal):
        if has_bias:
            bias = jnp.concatenate([bias_ref[b] for b in range(bias_ref.shape[0])], axis=1).astype(F32)
        if diagonal:
            causal = (lax.broadcasted_iota(jnp.int32, (t, t), 0)
                      <= lax.broadcasted_iota(jnp.int32, (t, t), 1))
        def logits(h):
            return lax.dot_general(k_ref[h], q_ref[h], _NT, preferred_element_type=F32)

        s_next = logits(0)
        for h in range(n_heads):
            s = s_next
            if h + 1 < n_heads:
                s_next = logits(h + 1)
            if has_bias:
                s = s + bias
            if diagonal:
                s = jnp.where(causal, s, NEG_INF)
            m_old = m_scr[h][0:1, :]
            m_new = jnp.maximum(m_old, jnp.max(s, axis=0, keepdims=True))
            m_safe = jnp.where(m_new == NEG_INF, 0.0, m_new)
            p = jnp.exp2(s - m_safe)
            alpha = jnp.exp2(m_old - m_safe)
            acc_scr[h] = alpha * acc_scr[h] + jnp.dot(vt_ref[h], p.astype(BF16), preferred_element_type=F32)
            m_scr[h] = jnp.broadcast_to(m_new, (sub, t))

    @pl.when(j < i)
    def _():
        step(False)

    @pl.when(j == i)
    def _():
        step(True)
        for hp in range(n_heads // 2):
            halves = []
            for h in (2 * hp, 2 * hp + 1):
                a = acc_scr[h]
                halves.append(a[0:HEAD_DIM, :] / a[HEAD_DIM:HEAD_DIM + 1, :])
            o_ref[:, hp * LANES:(hp + 1) * LANES] = jnp.concatenate(halves, axis=0).T.astype(o_ref.dtype)


def _attention(q, k, vt, bias, *, t):
    n_heads, s, _ = q.shape
    sub = 8
    in_specs = [pl.BlockSpec((n_heads, t, LANES), lambda i, j: (0, i, 0)),
                pl.BlockSpec((n_heads, t, LANES), lambda i, j: (0, jnp.minimum(j, i), 0)),
                pl.BlockSpec((n_heads, LANES, t), lambda i, j: (0, 0, jnp.minimum(j, i)))]
    args = [q, k, vt]
    if bias is not None:
        in_specs.append(pl.BlockSpec((t // LANES, t, LANES), lambda i, j: (i, jnp.minimum(j, i), 0)))
        args.append(bias)
    return pl.pallas_call(
        functools.partial(_attn_kernel, has_bias=bias is not None),
        grid=(s // t, s // t),
        in_specs=in_specs,
        out_specs=pl.BlockSpec((t, n_heads * HEAD_DIM), lambda i, j: (i, 0)),
        out_shape=jax.ShapeDtypeStruct((s, n_heads * HEAD_DIM), BF16),
        scratch_shapes=[pltpu.VMEM((n_heads, sub, t), F32), pltpu.VMEM((n_heads, LANES, t), F32)],
        compiler_params=_params(2),
        name="dsa_attention" if bias is not None else "fox_attention",
    )(*args)


def _matmul_res_kernel(*refs, n_pairs):
    res_ref = refs[2 * n_pairs]
    o_ref = refs[2 * n_pairs + 1]
    acc = res_ref[...]
    for p in range(n_pairs):
        acc = acc + jnp.dot(refs[2 * p][...], refs[2 * p + 1][...], preferred_element_type=F32)
    o_ref[...] = acc


def _matmul_res(pairs, res, *, tm):
    s, n = res.shape
    in_specs, args = [], []
    for a, w in pairs:
        in_specs += [pl.BlockSpec((tm, a.shape[1]), lambda i: (i, 0)),
                     pl.BlockSpec(w.shape, lambda i: (0, 0))]
        args += [a, w]
    in_specs.append(pl.BlockSpec((tm, n), lambda i: (i, 0)))
    args.append(res)
    return pl.pallas_call(
        functools.partial(_matmul_res_kernel, n_pairs=len(pairs)),
        grid=(s // tm,),
        in_specs=in_specs,
        out_specs=pl.BlockSpec((tm, n), lambda i: (i, 0)),
        out_shape=jax.ShapeDtypeStruct((s, n), F32),
        compiler_params=_params(1),
        name="matmul_residual",
    )(*args)


def _prep_odd_kernel(proj_ref, cos_ref, sin_ref, gq_ref, gk_ref, bmat_ref, q_ref, k_ref, v_ref):
    tm = proj_ref.shape[0]
    cos = cos_ref[...]
    sin = sin_ref[...]
    bmat = bmat_ref[...]
    lane = lax.broadcasted_iota(jnp.int32, (tm, LANES), 1)
    low_half = (lane & 32) == 0
    for blk in range(D_MODEL // LANES):
        cols = slice(blk * LANES, (blk + 1) * LANES)
        q = proj_ref[:, blk * LANES:(blk + 1) * LANES]
        k = proj_ref[:, D_MODEL + blk * LANES:D_MODEL + (blk + 1) * LANES]
        q_ref[:, cols] = _rope(_head_rmsnorm(q, gq_ref[...], bmat), cos, sin, low_half).astype(BF16)
        k_ref[:, cols] = _rope(_head_rmsnorm(k, gk_ref[...], bmat), cos, sin, low_half).astype(BF16)
        v_ref[:, cols] = proj_ref[:, 2 * D_MODEL + blk * LANES:2 * D_MODEL + (blk + 1) * LANES].astype(BF16)


def _prep_odd(proj, cos, sin, gq, gk, bmat, *, tm):
    s = proj.shape[0]
    row = lambda w: pl.BlockSpec((tm, w), lambda i: (i, 0))
    const = lambda a: pl.BlockSpec(a.shape, lambda i: (0, 0))
    shape = jax.ShapeDtypeStruct((s, D_MODEL), BF16)
    return pl.pallas_call(
        _prep_odd_kernel,
        grid=(s // tm,),
        in_specs=[row(3 * D_MODEL), row(LANES), row(LANES), const(gq), const(gk), const(bmat)],
        out_specs=[row(D_MODEL)] * 3,
        out_shape=[shape] * 3,
        compiler_params=_params(1),
        name="prep_odd",
    )(proj, cos, sin, gq, gk, bmat)


def _dilated_kernel(q_ref, kc_ref, kp_ref, vc_ref, vp_ref, o_ref, lse_ref):
    c = pl.program_id(1)
    t = q_ref.shape[0]
    qi = lax.broadcasted_iota(jnp.int32, (t, t), 0)
    kj = lax.broadcasted_iota(jnp.int32, (t, t), 1)
    cur_ok = kj <= qi
    prev_ok = (kj >= qi) & (c > 0)
    lane = lax.broadcasted_iota(jnp.int32, (t, LANES), 1)
    lse_blk = jnp.zeros((t, LANES), F32)
    for h in range(N_HEADS):
        cols = slice(h * HEAD_DIM, (h + 1) * HEAD_DIM)
        q = q_ref[:, cols]
        s_cur = jnp.where(cur_ok, lax.dot_general(q, kc_ref[:, cols], _NT, preferred_element_type=F32), NEG_INF)
        s_prev = jnp.where(prev_ok, lax.dot_general(q, kp_ref[:, cols], _NT, preferred_element_type=F32), NEG_INF)
        m = jnp.maximum(jnp.max(s_cur, axis=-1, keepdims=True), jnp.max(s_prev, axis=-1, keepdims=True))
        e_cur = jnp.exp2(s_cur - m)
        e_prev = jnp.exp2(s_prev - m)
        den = jnp.sum(e_cur, axis=-1, keepdims=True) + jnp.sum(e_prev, axis=-1, keepdims=True)
        o = (jnp.dot(e_cur.astype(BF16), vc_ref[:, cols], preferred_element_type=F32)
             + jnp.dot(e_prev.astype(BF16), vp_ref[:, cols], preferred_element_type=F32))
        o_ref[:, cols] = o / den
        lse_blk = jnp.where(lane == h, m + jnp.log2(den), lse_blk)
    lse_ref[...] = lse_blk


def _dilated(q, k, v, dil):
    s = q.shape[0]
    rows = s // dil
    q2, k2, v2 = (a.reshape(rows, dil * D_MODEL) for a in (q, k, v))
    cur = pl.BlockSpec((Q_BLOCK, D_MODEL), lambda r, c: (c, r))
    prev = pl.BlockSpec((Q_BLOCK, D_MODEL), lambda r, c: (jnp.maximum(c - 1, 0), r))
    o, lse = pl.pallas_call(
        _dilated_kernel,
        grid=(dil, rows // Q_BLOCK),
        in_specs=[cur, cur, prev, cur, prev],
        out_specs=[cur, pl.BlockSpec((Q_BLOCK, LANES), lambda r, c: (c, r))],
        out_shape=[jax.ShapeDtypeStruct((rows, dil * D_MODEL), F32),
                   jax.ShapeDtypeStruct((rows, dil * LANES), F32)],
        compiler_params=_params(2),
        name=f"dilated_d{dil}",
    )(q2, k2, k2, v2, v2)
    return o.reshape(s, D_MODEL), lse.reshape(s, LANES)


def _merge_out_kernel(o1_ref, o2_ref, o3_ref, l1_ref, l2_ref, l3_ref, e_ref, w_ref, res_ref, out_ref):
    l1, l2, l3 = l1_ref[...], l2_ref[...], l3_ref[...]
    m = jnp.maximum(jnp.maximum(l1, l2), l3)
    w1, w2, w3 = jnp.exp2(l1 - m), jnp.exp2(l2 - m), jnp.exp2(l3 - m)
    inv = 1.0 / (w1 + w2 + w3)
    e = e_ref[...]

    def expand(a):
        hi = a.astype(BF16)
        lo = (a - hi.astype(F32)).astype(BF16)
        return jnp.dot(hi, e, preferred_element_type=F32) + jnp.dot(lo, e, preferred_element_type=F32)

    merged = (expand(w1 * inv) * o1_ref[...] + expand(w2 * inv) * o2_ref[...]
              + expand(w3 * inv) * o3_ref[...])
    out_ref[...] = res_ref[...] + jnp.dot(merged.astype(BF16), w_ref[...], preferred_element_type=F32)


def _merge_out(os_, lses, expand, w, res, *, tm):
    s = res.shape[0]
    row = lambda width: pl.BlockSpec((tm, width), lambda i: (i, 0))
    const = lambda a: pl.BlockSpec(a.shape, lambda i: (0, 0))
    return pl.pallas_call(
        _merge_out_kernel,
        grid=(s // tm,),
        in_specs=[row(D_MODEL)] * 3 + [row(LANES)] * 3 + [const(expand), const(w), row(D_MODEL)],
        out_specs=row(D_MODEL),
        out_shape=jax.ShapeDtypeStruct((s, D_MODEL), F32),
        compiler_params=_params(1),
        name="dilated_merge_out",
    )(*os_, *lses, expand, w, res)


def _gelu_tanh(x):
    return 0.5 * x * (1.0 + jnp.tanh(np.sqrt(2.0 / np.pi).astype(np.float32) * (x + 0.044715 * (x * x * x))))


def _ffn_act_kernel(x_ref, halo_ref, g_ref, wg_ref, wu_ref, cw_ref, cb_ref, o_ref):
    i = pl.program_id(1)
    tm = x_ref.shape[0]
    h = _rmsnorm_rows(x_ref[...], g_ref[...]).astype(BF16)
    hh = _rmsnorm_rows(halo_ref[...], g_ref[...]).astype(BF16)
    wg = wg_ref[...]
    gate = jnp.dot(h, wg, preferred_element_type=F32)
    gate_halo = jnp.where(i > 0, jnp.dot(hh, wg, preferred_element_type=F32), 0.0)
    up = jnp.dot(h, wu_ref[...], preferred_element_type=F32)
    row = lax.broadcasted_iota(jnp.int32, gate.shape, 0)
    g_m1 = jnp.where(row == 0, gate_halo[7:8, :], pltpu.roll(gate, 1, 0))
    g_m2 = jnp.where(row == 0, gate_halo[6:7, :],
                     jnp.where(row == 1, gate_halo[7:8, :], pltpu.roll(gate, 2, 0)))
    cw = cw_ref[...]
    conv = cw[2:3, :] * gate + cw[1:2, :] * g_m1 + cw[0:1, :] * g_m2 + cb_ref[...]
    o_ref[...] = (_gelu_tanh(conv) * up).astype(o_ref.dtype)
    del tm


def _ffn_act(x, g, wg, wu, cw, cb, *, tm, tn):
    s, d = x.shape
    n = wg.shape[1]
    halo_rows = 8
    return pl.pallas_call(
        _ffn_act_kernel,
        grid=(n // tn, s // tm),
        in_specs=[pl.BlockSpec((tm, d), lambda j, i: (i, 0)),
                  pl.BlockSpec((halo_rows, d), lambda j, i: (jnp.maximum(i * (tm // halo_rows) - 1, 0), 0)),
                  pl.BlockSpec((1, d), lambda j, i: (0, 0)),
                  pl.BlockSpec((d, tn), lambda j, i: (0, j)),
                  pl.BlockSpec((d, tn), lambda j, i: (0, j)),
                  pl.BlockSpec((cw.shape[0], tn), lambda j, i: (0, j)),
                  pl.BlockSpec((1, tn), lambda j, i: (0, j))],
        out_specs=pl.BlockSpec((tm, tn), lambda j, i: (i, j)),
        out_shape=jax.ShapeDtypeStruct((s, n), BF16),
        compiler_params=_params(2),
        name="ffn_act",
    )(x, x, g, wg, wu, cw, cb)


def _ple_kernel(x_ref, p_ref, g_ref, wg_ref, wp_ref, o_ref):
    x = x_ref[...]
    z = jnp.dot(_rmsnorm_rows(x, g_ref[...]).astype(BF16), wg_ref[...], preferred_element_type=F32)
    gate = 1.0 / (1.0 + jnp.exp(-z))
    proj = jnp.dot(p_ref[...].astype(BF16), wp_ref[...], preferred_element_type=F32)
    o_ref[...] = x + gate * proj


def _ple(x, p, g, wg, wp, *, tm):
    s, d = x.shape
    return pl.pallas_call(
        _ple_kernel,
        grid=(s // tm,),
        in_specs=[pl.BlockSpec((tm, d), lambda i: (i, 0)),
                  pl.BlockSpec((tm, p.shape[1]), lambda i: (i, 0)),
                  pl.BlockSpec((1, d), lambda i: (0, 0)),
                  pl.BlockSpec(wg.shape, lambda i: (0, 0)),
                  pl.BlockSpec(wp.shape, lambda i: (0, 0))],
        out_specs=pl.BlockSpec((tm, d), lambda i: (i, 0)),
        out_shape=jax.ShapeDtypeStruct((s, d), F32),
        compiler_params=_params(1),
        name="ple",
    )(x, p, g, wg, wp)


def _rope_tables(s):
    half = HEAD_DIM // 2
    inv_freq = ROPE_THETA ** (-jnp.arange(half, dtype=F32) / half)
    ang = jnp.arange(s).astype(F32)[:, None] * inv_freq[None, :]
    cos, sin = jnp.cos(ang), jnp.sin(ang)
    return jnp.tile(cos, (1, 4)), jnp.tile(jnp.concatenate([-sin, sin], axis=1), (1, 2))


def _gain_tile(g, scale=1.0):
    return (jnp.tile(g.astype(F32), 2) * scale)[None, :]


def _even_w_in(w):
    hd = N_HEADS_A * HEAD_DIM
    cuts = np.cumsum([hd, hd, hd, N_HEADS_A, hd, hd, hd, IDX_HEADS * IDX_DIM, IDX_DIM, IDX_HEADS])[:-1]
    qa, ka, va, fa, qb, kb, vb, qi, ki, wi = jnp.split(w, [int(c) for c in cuts], axis=1)
    pad = jnp.zeros((w.shape[0], LANES - IDX_DIM - N_HEADS_A - IDX_HEADS), w.dtype)
    return jnp.concatenate([qa, ka, va, qb, kb, vb, qi, ki, fa, wi, pad], axis=1).astype(BF16)


def kernel(x, p, norm_mix, w_in_even, b_forget, q_norm_a, k_norm_a, q_norm_b, k_norm_b, w_in_odd, q_norm_c,
           k_norm_c, w_out, norm_ffn, w_gate, w_up, conv_w, conv_b, w_down, norm_ple, w_ple_gate, w_ple_proj):
    batch, s, d = x.shape
    assert batch == 1 and d == D_MODEL
    depth = norm_mix.shape[0]
    topk = min(TOPK_MAX, s // 4)
    t_attn = min(512, s)
    tm = min(512, s)
    tm_prep = min(256, s)
    qscale = HEAD_DIM ** -0.5 * LOG2E

    cos, sin = _rope_tables(s)
    lane = np.arange(LANES)
    bmat = jnp.asarray((lane[:, None] // HEAD_DIM == lane[None, :] // HEAD_DIM) / HEAD_DIM, BF16)
    tri = jnp.asarray(np.tril(np.ones((tm_prep, tm_prep))), BF16)
    expand = jnp.asarray(lane[:, None] == np.arange(D_MODEL)[None, :] // HEAD_DIM, BF16)

    xs = x[0]
    for i in range(depth):
        j = i // 2
        g_mix = norm_mix[i][None, :]
        w_o = w_out[i].astype(BF16)
        if i % 2 == 0:
            proj = _rms_matmul(xs, g_mix, _even_w_in(w_in_even[j]), tm=tm, tn=EV_WIDTH // 3)
            bf_pad = jnp.zeros((1, LANES), F32).at[0, MISC_F:MISC_F + N_HEADS_A].set(b_forget[j])
            qa, ka, va_t, qb, kb, vb_t, qi, ki, wi_t = _prep_even(
                proj, cos, sin, _gain_tile(q_norm_a[j], qscale), _gain_tile(k_norm_a[j]),
                _gain_tile(q_norm_b[j], qscale), _gain_tile(k_norm_b[j]), bf_pad, bmat, tri, tm=tm_prep)
            out_a = _attention(qa, ka, va_t, None, t=t_attn)
            bias = _indexer(qi, ki, wi_t, tq=LANES, tk=t_attn, topk=topk, group=16)
            out_b = _attention(qb, kb, vb_t, bias, t=t_attn)
            half = N_HEADS_A * HEAD_DIM
            xs = _matmul_res([(out_a, w_o[:half]), (out_b, w_o[half:])], xs, tm=tm)
        else:
            proj = _rms_matmul(xs, g_mix, w_in_odd[j].astype(BF16), tm=tm, tn=D_MODEL)
            q, k, v = _prep_odd(proj, cos, sin, _gain_tile(q_norm_c[j], qscale), _gain_tile(k_norm_c[j]),
                                bmat, tm=tm_prep)
            outs, lses = zip(*[_dilated(q, k, v, dil) for _, dil in DIL_PATTERNS])
            xs = _merge_out(outs, lses, expand, w_o, xs, tm=tm_prep)
        act = _ffn_act(xs, norm_ffn[i][None, :], w_gate[i].astype(BF16), w_up[i].astype(BF16),
                       conv_w[i], conv_b[i][None, :], tm=tm_prep, tn=D_FF // 2)
        xs = _matmul_res([(act, w_down[i].astype(BF16))], xs, tm=tm)
        xs = _ple(xs, p[i, 0], norm_ple[i][None, :], w_ple_gate[i].astype(BF16),
                  w_ple_proj[i].astype(BF16), tm=tm)
    return xs[None]
```

```python
import functools

import numpy as np
import jax
import jax.numpy as jnp
from jax import lax
from jax.experimental import pallas as pl
from jax.experimental.pallas import tpu as pltpu

F32 = jnp.float32
BF16 = jnp.bfloat16

D_MODEL = 1024
HEAD_DIM = 64
N_HEADS = D_MODEL // HEAD_DIM
N_HEADS_A = N_HEADS // 2
N_HEADS_B = N_HEADS - N_HEADS_A
IDX_HEADS = 16
IDX_DIM = 64
TOPK_MAX = 256
DIL_PATTERNS = ((128, 1), (512, 4), (2048, 16))
Q_BLOCK = 128
D_FF = 2816
P_DIM = 256
ROPE_THETA = 10000.0
EPS = 1e-6
DEPTH = 4

LANES = 128
LOG2E = 1.4426950408889634
NEG_INF = float("-inf")
INT_MIN = -(2 ** 31)
VMEM_LIMIT_BYTES = 56 * 1024 * 1024
LOGIT_CEIL_LOG2 = 40.0
LOGIT_BOUND_MAX = 80.0

EV_QA, EV_KA, EV_VA, EV_QB, EV_KB, EV_VB, EV_QI, EV_MISC = 0, 512, 1024, 1536, 2048, 2560, 3072, 4096
EV_WIDTH = 4224
MISC_F = 64
MISC_W = 72

_NT = (((1,), (1,)), ((), ()))


def _params(n_axes):
    return pltpu.CompilerParams(dimension_semantics=("arbitrary",) * n_axes,
                                vmem_limit_bytes=VMEM_LIMIT_BYTES)


def _rmsnorm_rows(x, g):
    ms = jnp.mean(x * x, axis=-1, keepdims=True)
    return x * lax.rsqrt(ms + EPS) * g


def _split3(x):
    hi = x.astype(BF16)
    r = x - hi.astype(F32)
    mid = r.astype(BF16)
    lo = (r - mid.astype(F32)).astype(BF16)
    return hi, mid, lo


def _rms_matmul_kernel(x_ref, g_ref, w_ref, o_ref):
    h = _rmsnorm_rows(x_ref[...], g_ref[...]).astype(BF16)
    o_ref[...] = jnp.dot(h, w_ref[...], preferred_element_type=F32)


def _rms_matmul(x, g, w, *, tm, tn):
    s, d = x.shape
    n = w.shape[1]
    return pl.pallas_call(
        _rms_matmul_kernel,
        grid=(n // tn, s // tm),
        in_specs=[pl.BlockSpec((tm, d), lambda j, i: (i, 0)),
                  pl.BlockSpec((1, d), lambda j, i: (0, 0)),
                  pl.BlockSpec((d, tn), lambda j, i: (0, j))],
        out_specs=pl.BlockSpec((tm, tn), lambda j, i: (i, j)),
        out_shape=jax.ShapeDtypeStruct((s, n), F32),
        compiler_params=_params(2),
        name="rms_matmul",
    )(x, g, w)


def _head_rmsnorm(x, gain, bmat):
    sq = x * x
    hi = sq.astype(BF16)
    lo = (sq - hi.astype(F32)).astype(BF16)
    ms = (jnp.dot(hi, bmat, preferred_element_type=F32)
          + jnp.dot(lo, bmat, preferred_element_type=F32))
    return x * lax.rsqrt(ms + EPS) * gain


def _rope(x, cos, sin_signed, low_half):
    rot = jnp.where(low_half, pltpu.roll(x, LANES - 32, 1), pltpu.roll(x, 32, 1))
    return x * cos + rot * sin_signed


def _log_sigmoid(z):
    return jnp.minimum(z, 0.0) - jnp.log1p(jnp.exp(-jnp.abs(z)))


def _prep_even_kernel(proj_ref, cos_ref, sin_ref, gqa_ref, gka_ref, gqb_ref, gkb_ref, bf_ref,
                      shift_ref, bmat_ref, tri_ref,
                      qa_ref, ka_ref, va_ref, qb_ref, kb_ref, vb_ref, qi_ref, ki_ref, wi_ref, carry):
    tm = proj_ref.shape[0]
    shift_a = shift_ref[0:1, :]
    shift_b = shift_ref[1:2, :]

    @pl.when(pl.program_id(0) == 0)
    def _():
        carry[...] = jnp.zeros_like(carry)

    cos = cos_ref[...]
    sin = sin_ref[...]
    bmat = bmat_ref[...]
    lane = lax.broadcasted_iota(jnp.int32, (tm, LANES), 1)
    low_half = (lane & 32) == 0
    head0 = lane < HEAD_DIM

    misc = proj_ref[:, EV_MISC:EV_MISC + LANES]

    logf = _log_sigmoid(misc + bf_ref[...])
    logf = jnp.where((lane >= MISC_F) & (lane < MISC_F + N_HEADS_A), logf, 0.0)
    tri = tri_ref[...]
    f_hi, f_mid, f_lo = _split3(logf)
    c = (jnp.dot(tri, f_hi, preferred_element_type=F32)
         + jnp.dot(tri, f_mid, preferred_element_type=F32)
         + jnp.dot(tri, f_lo, preferred_element_type=F32)) + carry[0:1, :]
    carry[0:1, :] = c[tm - 1:tm, :]
    c_hi, c_mid, c_lo = (t.astype(F32) for t in _split3(c * LOG2E))

    def place(block, h):
        return block if h % 2 == 0 else pltpu.roll(block, HEAD_DIM, 1)

    for hp in range(N_HEADS_A // 2):
        cols = slice(hp * LANES, (hp + 1) * LANES)
        qa = _head_rmsnorm(proj_ref[:, EV_QA + hp * LANES:EV_QA + (hp + 1) * LANES], gqa_ref[...], bmat)
        ka = _head_rmsnorm(proj_ref[:, EV_KA + hp * LANES:EV_KA + (hp + 1) * LANES], gka_ref[...], bmat)
        va = proj_ref[:, EV_VA + hp * LANES:EV_VA + (hp + 1) * LANES]
        qb = _rope(_head_rmsnorm(proj_ref[:, EV_QB + hp * LANES:EV_QB + (hp + 1) * LANES], gqb_ref[...], bmat),
                   cos, sin, low_half)
        kb = _rope(_head_rmsnorm(proj_ref[:, EV_KB + hp * LANES:EV_KB + (hp + 1) * LANES], gkb_ref[...], bmat),
                   cos, sin, low_half)
        vb = proj_ref[:, EV_VB + hp * LANES:EV_VB + (hp + 1) * LANES]
        del cols
        for sub in range(2):
            h = 2 * hp + sub
            src = MISC_F + h
            q_aug = jnp.where(
                head0, place(qa, h),
                jnp.where(lane == 64, pltpu.roll(c_hi, (64 - src) % LANES, 1),
                          jnp.where(lane == 65, pltpu.roll(c_mid, (65 - src) % LANES, 1),
                                    jnp.where(lane == 66, pltpu.roll(c_lo, (66 - src) % LANES, 1),
                                              jnp.where(lane < 70, 1.0,
                                                        jnp.where(lane == 70, shift_a, 0.0))))))
            k_aug = jnp.where(
                head0, place(ka, h),
                jnp.where(lane < 67, 1.0,
                          jnp.where(lane == 67, -pltpu.roll(c_hi, (67 - src) % LANES, 1),
                                    jnp.where(lane == 68, -pltpu.roll(c_mid, (68 - src) % LANES, 1),
                                              jnp.where(lane == 69, -pltpu.roll(c_lo, (69 - src) % LANES, 1),
                                                        jnp.where(lane == 70, 1.0, 0.0))))))
            ones_col = jnp.where(lane == HEAD_DIM, 1.0, 0.0)
            qa_ref[h] = q_aug.astype(BF16)
            ka_ref[h] = k_aug.astype(BF16)
            va_ref[h] = jnp.where(head0, place(va, h), ones_col).T.astype(BF16)
            qb_ref[h] = jnp.where(head0, place(qb, h), jnp.where(lane == HEAD_DIM, shift_b, 0.0)).astype(BF16)
            kb_ref[h] = jnp.where(head0, place(kb, h), ones_col).astype(BF16)
            vb_ref[h] = jnp.where(head0, place(vb, h), ones_col).T.astype(BF16)

    for blk in range(IDX_HEADS * IDX_DIM // LANES):
        cols = slice(EV_QI + blk * LANES, EV_QI + (blk + 1) * LANES)
        qi = _rope(proj_ref[:, cols], cos, sin, low_half) * (IDX_DIM ** -0.5)
        qi_ref[2 * blk] = qi[:, 0:IDX_DIM].astype(BF16)
        qi_ref[2 * blk + 1] = qi[:, IDX_DIM:2 * IDX_DIM].astype(BF16)
    ki_ref[...] = _rope(misc, cos, sin, low_half)[:, 0:IDX_DIM].astype(BF16)
    wi_ref[...] = (misc * (IDX_HEADS ** -0.5)).T


def _prep_even(proj, cos, sin, gqa, gka, gqb, gkb, bf_pad, shifts, bmat, tri, *, tm):
    s = proj.shape[0]
    row = lambda w: pl.BlockSpec((tm, w), lambda i: (i, 0))
    const = lambda a: pl.BlockSpec(a.shape, lambda i: (0, 0))
    heads = pl.BlockSpec((N_HEADS_A, tm, LANES), lambda i: (0, i, 0))
    heads_shape = jax.ShapeDtypeStruct((N_HEADS_A, s, LANES), BF16)
    heads_t = pl.BlockSpec((N_HEADS_A, LANES, tm), lambda i: (0, 0, i))
    heads_t_shape = jax.ShapeDtypeStruct((N_HEADS_A, LANES, s), BF16)
    return pl.pallas_call(
        _prep_even_kernel,
        grid=(s // tm,),
        in_specs=[row(EV_WIDTH), row(LANES), row(LANES), const(gqa), const(gka), const(gqb), const(gkb),
                  const(bf_pad), const(shifts), const(bmat), const(tri)],
        out_specs=[heads, heads, heads_t, heads, heads, heads_t,
                   pl.BlockSpec((IDX_HEADS, tm, IDX_DIM), lambda i: (0, i, 0)),
                   row(IDX_DIM),
                   pl.BlockSpec((LANES, tm), lambda i: (0, i))],
        out_shape=[heads_shape, heads_shape, heads_t_shape, heads_shape, heads_shape, heads_t_shape,
                   jax.ShapeDtypeStruct((IDX_HEADS, s, IDX_DIM), BF16),
                   jax.ShapeDtypeStruct((s, IDX_DIM), BF16),
                   jax.ShapeDtypeStruct((LANES, s), F32)],
        scratch_shapes=[pltpu.VMEM((8, LANES), F32)],
        compiler_params=_params(1),
        name="prep_even",
    )(proj, cos, sin, gqa, gka, gqb, gkb, bf_pad, shifts, bmat, tri)


def _float_key(x):
    bits = pltpu.bitcast(x, jnp.int32)
    return bits ^ ((bits >> 31) & 0x7FFFFFFF)


def _key_float(k):
    return pltpu.bitcast(k ^ ((k >> 31) & 0x7FFFFFFF), F32)


def _kth_search(count_fn, lo_k, hi_k, c_lo, done, kf, *, fixed_steps=None, steps_per_check=1):
    def finished(lo, hi, c, d):
        return jnp.where(d | (c == kf) | (lo + 1 >= hi), 1.0, 0.0)

    def halve(lo, hi, c, d):
        live = d == 0.0
        cand = (lo >> 1) + (hi >> 1) + (lo & hi & 1)
        n = count_fn(_key_float(cand))
        take = (n >= kf) & live
        drop = (n < kf) & live
        lo = jnp.where(take, cand, lo)
        c = jnp.where(take, n, c)
        hi = jnp.where(drop, cand, hi)
        return lo, hi, c, finished(lo, hi, c, d > 0.0)

    d0 = finished(lo_k, hi_k, c_lo, done)
    if fixed_steps is not None:
        lo_k, _, c_lo, _ = lax.fori_loop(0, fixed_steps, lambda _, st: halve(*st), (lo_k, hi_k, c_lo, d0))
        return lo_k, c_lo

    def body(state):
        st = state[:4]
        for _ in range(steps_per_check):
            st = halve(*st)
        return (*st, jnp.max(1.0 - st[3]))

    lo_k, _, c_lo, _, _ = lax.while_loop(lambda state: state[4] > 0.0, body,
                                         (lo_k, hi_k, c_lo, d0, jnp.max(1.0 - d0)))
    return lo_k, c_lo


def _indexer_kernel(qi_ref, ki_ref, wi_ref, bias_ref, sc_scr, g_scr, *, tq, tk, topk, group):
    i = pl.program_id(0)
    s_keys = sc_scr.shape[0]
    n_chunks = s_keys // tk
    nk = ((i + 1) * tq + tk - 1) // tk
    sub = 8
    g_rows = tk // group
    kf = float(topk)
    lowest = float(np.finfo(np.float32).min)
    qpos = lax.broadcasted_iota(jnp.int32, (1, tq), 1) + i * tq
    w = wi_ref[...]

    g_scr[...] = jnp.full(g_scr.shape, NEG_INF, F32)

    def score_chunk(c, gmax):
        ks = pl.multiple_of(c * tk, tk)
        ki = ki_ref[pl.ds(ks, tk), :]
        acc = jnp.zeros((tk, tq), F32)
        for hp in range(IDX_HEADS // 2):
            q2 = jnp.concatenate([qi_ref[2 * hp], qi_ref[2 * hp + 1]], axis=0)
            l01 = lax.dot_general(ki, q2, _NT, preferred_element_type=F32)
            w0 = w[MISC_W + 2 * hp:MISC_W + 2 * hp + 1, :]
            w1 = w[MISC_W + 2 * hp + 1:MISC_W + 2 * hp + 2, :]
            acc = acc + w0 * jnp.maximum(l01[:, 0:tq], 0.0) + w1 * jnp.maximum(l01[:, tq:2 * tq], 0.0)
        kpos = lax.broadcasted_iota(jnp.int32, (tk, tq), 0) + c * tk
        sc = jnp.where(kpos <= qpos, acc, NEG_INF)
        sc_scr[pl.ds(ks, tk), :] = sc
        g = jnp.max(sc.reshape(tk // (group * sub), group, sub, tq), axis=1).reshape(g_rows, tq)
        g_scr[pl.ds(pl.multiple_of(c * g_rows, g_rows), g_rows), :] = g
        return jnp.maximum(gmax, jnp.max(g, axis=0, keepdims=True))

    gmax = lax.fori_loop(0, nk, score_chunk, jnp.full((1, tq), NEG_INF, F32))

    def count_rows(ref, rows_per_step, n_steps, cand):
        ways = min(8, rows_per_step // sub)

        def step(c, cnt):
            r0 = pl.multiple_of(c * rows_per_step, rows_per_step)
            ones = jnp.where(ref[pl.ds(r0, rows_per_step), :] >= cand, 1.0, 0.0)
            return cnt + jnp.sum(ones.reshape(rows_per_step // (ways * sub), ways, sub, tq), axis=0)

        cnt = lax.fori_loop(0, n_steps, step, jnp.zeros((ways, sub, tq), F32))
        return jnp.sum(jnp.sum(cnt, axis=0), axis=0, keepdims=True)

    count_groups = lambda cand: count_rows(g_scr, g_rows, nk, cand)
    count_all = lambda cand: count_rows(sc_scr, tk, nk, cand)

    lo0 = _float_key(jnp.full((1, tq), lowest, F32))
    hi0 = _float_key(gmax) + 1
    few = qpos < topk
    c_groups = count_groups(jnp.full((1, tq), lowest, F32))
    u_k, _ = _kth_search(count_groups, lo0, hi0, c_groups, few | (c_groups < kf), kf, fixed_steps=32)
    c_u = count_all(_key_float(u_k))
    t_k, _ = _kth_search(count_all, u_k, hi0, c_u, few, kf, steps_per_check=2)
    thr = jnp.where(few, lowest, _key_float(t_k))

    def bias_chunk(c, carry):
        ks = pl.multiple_of(c * tk, tk)
        bias_ref[pl.ds(ks, tk), :] = jnp.where(sc_scr[pl.ds(ks, tk), :] >= thr, 1.0, 0.0).astype(BF16)
        return carry

    lax.fori_loop(0, nk, bias_chunk, 0)

    def fill_chunk(c, carry):
        bias_ref[pl.ds(pl.multiple_of(c * tk, tk), tk), :] = jnp.zeros((tk, tq), BF16)
        return carry

    lax.fori_loop(nk, n_chunks, fill_chunk, 0)


def _indexer(qi, ki, wi_t, *, tq, tk, topk, group):
    s = ki.shape[0]
    return pl.pallas_call(
        functools.partial(_indexer_kernel, tq=tq, tk=tk, topk=topk, group=group),
        grid=(s // tq,),
        in_specs=[pl.BlockSpec((IDX_HEADS, tq, IDX_DIM), lambda i: (0, i, 0)),
                  pl.BlockSpec((s, IDX_DIM), lambda i: (0, 0)),
                  pl.BlockSpec((LANES, tq), lambda i: (0, i))],
        out_specs=pl.BlockSpec((None, s, tq), lambda i: (i, 0, 0)),
        out_shape=jax.ShapeDtypeStruct((s // tq, s, tq), BF16),
        scratch_shapes=[pltpu.VMEM((s, tq), F32), pltpu.VMEM((s // group, tq), F32)],
        compiler_params=_params(1),
        name="dsa_indexer",
    )(qi, ki, wi_t)


def _attn_kernel(*refs, has_mask, bounded):
    q_ref, k_ref, vt_ref = refs[:3]
    mask_ref = refs[3] if has_mask else None
    o_ref = refs[3 + has_mask]
    scratch = refs[4 + has_mask:]
    acc_scr = scratch[-1]
    m_scr = None if bounded else scratch[0]
    n_heads, t, _ = q_ref.shape
    i = pl.program_id(0)
    j = pl.program_id(1)

    @pl.when(j == 0)
    def _():
        acc_scr[...] = jnp.zeros(acc_scr.shape, F32)
        if not bounded:
            m_scr[...] = jnp.full(m_scr.shape, NEG_INF, F32)

    def step(diagonal):
        if has_mask:
            mask = jnp.concatenate([mask_ref[b] for b in range(mask_ref.shape[0])], axis=1)
            if not bounded:
                selected = mask.astype(F32) > 0.0
        if diagonal:
            causal = (lax.broadcasted_iota(jnp.int32, (t, t), 0)
                      <= lax.broadcasted_iota(jnp.int32, (t, t), 1))

        def logits(h):
            return lax.dot_general(k_ref[h], q_ref[h], _NT, preferred_element_type=F32)

        s_next = logits(0)
        for h in range(n_heads):
            s = s_next
            if h + 1 < n_heads:
                s_next = logits(h + 1)
            if diagonal:
                s = jnp.where(causal, s, NEG_INF)
            if bounded:
                p = jnp.exp2(s).astype(BF16)
                if has_mask:
                    p = p * mask
                acc_scr[h] = acc_scr[h] + jnp.dot(vt_ref[h], p, preferred_element_type=F32)
            else:
                if has_mask:
                    s = jnp.where(selected, s, NEG_INF)
                m_old = m_scr[h][0:1, :]
                m_new = jnp.maximum(m_old, jnp.max(s, axis=0, keepdims=True))
                m_safe = jnp.where(m_new == NEG_INF, 0.0, m_new)
                p = jnp.exp2(s - m_safe)
                alpha = jnp.exp2(m_old - m_safe)
                acc_scr[h] = alpha * acc_scr[h] + jnp.dot(vt_ref[h], p.astype(BF16), preferred_element_type=F32)
                m_scr[h] = jnp.broadcast_to(m_new, m_scr.shape[1:])

    @pl.when(j < i)
    def _():
        step(False)

    @pl.when(j == i)
    def _():
        step(True)
        for hp in range(n_heads // 2):
            halves = []
            for h in (2 * hp, 2 * hp + 1):
                a = acc_scr[h]
                halves.append(a[0:HEAD_DIM, :] / a[HEAD_DIM:HEAD_DIM + 1, :])
            o_ref[:, hp * LANES:(hp + 1) * LANES] = jnp.concatenate(halves, axis=0).T.astype(o_ref.dtype)


def _attention(q, k, vt, mask, *, t, bounded):
    n_heads, s, _ = q.shape
    sub = 8
    in_specs = [pl.BlockSpec((n_heads, t, LANES), lambda i, j: (0, i, 0)),
                pl.BlockSpec((n_heads, t, LANES), lambda i, j: (0, jnp.minimum(j, i), 0)),
                pl.BlockSpec((n_heads, LANES, t), lambda i, j: (0, 0, jnp.minimum(j, i)))]
    args = [q, k, vt]
    if mask is not None:
        in_specs.append(pl.BlockSpec((t // LANES, t, LANES), lambda i, j: (i, jnp.minimum(j, i), 0)))
        args.append(mask)
    scratch = [pltpu.VMEM((n_heads, LANES, t), F32)]
    if not bounded:
        scratch.insert(0, pltpu.VMEM((n_heads, sub, t), F32))
    name = ("dsa_attention" if mask is not None else "fox_attention") + ("" if bounded else "_online")
    return pl.pallas_call(
        functools.partial(_attn_kernel, has_mask=mask is not None, bounded=bounded),
        grid=(s // t, s // t),
        in_specs=in_specs,
        out_specs=pl.BlockSpec((t, n_heads * HEAD_DIM), lambda i, j: (i, 0)),
        out_shape=jax.ShapeDtypeStruct((s, n_heads * HEAD_DIM), BF16),
        scratch_shapes=scratch,
        compiler_params=_params(2),
        name=name,
    )(*args)


def _attention_any(q, k, vt, mask, logit_bound, *, t):
    operands = (q, k, vt) if mask is None else (q, k, vt, mask)
    m = lambda ops: ops[3] if mask is not None else None
    return lax.cond(logit_bound <= LOGIT_BOUND_MAX,
                    lambda *ops: _attention(ops[0], ops[1], ops[2], m(ops), t=t, bounded=True),
                    lambda *ops: _attention(ops[0], ops[1], ops[2], m(ops), t=t, bounded=False),
                    *operands)


def _matmul_res_kernel(*refs, n_pairs):
    res_ref = refs[2 * n_pairs]
    o_ref = refs[2 * n_pairs + 1]
    acc = res_ref[...]
    for p in range(n_pairs):
        acc = acc + jnp.dot(refs[2 * p][...], refs[2 * p + 1][...], preferred_element_type=F32)
    o_ref[...] = acc


def _matmul_res(pairs, res, *, tm):
    s, n = res.shape
    in_specs, args = [], []
    for a, w in pairs:
        in_specs += [pl.BlockSpec((tm, a.shape[1]), lambda i: (i, 0)),
                     pl.BlockSpec(w.shape, lambda i: (0, 0))]
        args += [a, w]
    in_specs.append(pl.BlockSpec((tm, n), lambda i: (i, 0)))
    args.append(res)
    return pl.pallas_call(
        functools.partial(_matmul_res_kernel, n_pairs=len(pairs)),
        grid=(s // tm,),
        in_specs=in_specs,
        out_specs=pl.BlockSpec((tm, n), lambda i: (i, 0)),
        out_shape=jax.ShapeDtypeStruct((s, n), F32),
        compiler_params=_params(1),
        name="matmul_residual",
    )(*args)


def _prep_odd_kernel(proj_ref, cos_ref, sin_ref, gq_ref, gk_ref, bmat_ref, q_ref, k_ref, v_ref):
    tm = proj_ref.shape[0]
    cos = cos_ref[...]
    sin = sin_ref[...]
    bmat = bmat_ref[...]
    lane = lax.broadcasted_iota(jnp.int32, (tm, LANES), 1)
    low_half = (lane & 32) == 0
    for blk in range(D_MODEL // LANES):
        cols = slice(blk * LANES, (blk + 1) * LANES)
        q = proj_ref[:, blk * LANES:(blk + 1) * LANES]
        k = proj_ref[:, D_MODEL + blk * LANES:D_MODEL + (blk + 1) * LANES]
        q_ref[:, cols] = _rope(_head_rmsnorm(q, gq_ref[...], bmat), cos, sin, low_half)
        k_ref[:, cols] = _rope(_head_rmsnorm(k, gk_ref[...], bmat), cos, sin, low_half)
        v_ref[:, cols] = proj_ref[:, 2 * D_MODEL + blk * LANES:2 * D_MODEL + (blk + 1) * LANES]


def _prep_odd(proj, cos, sin, gq, gk, bmat, *, tm):
    s = proj.shape[0]
    row = lambda w: pl.BlockSpec((tm, w), lambda i: (i, 0))
    const = lambda a: pl.BlockSpec(a.shape, lambda i: (0, 0))
    shape = jax.ShapeDtypeStruct((s, D_MODEL), F32)
    return pl.pallas_call(
        _prep_odd_kernel,
        grid=(s // tm,),
        in_specs=[row(3 * D_MODEL), row(LANES), row(LANES), const(gq), const(gk), const(bmat)],
        out_specs=[row(D_MODEL)] * 3,
        out_shape=[shape] * 3,
        compiler_params=_params(1),
        name="prep_odd",
    )(proj, cos, sin, gq, gk, bmat)


def _band_head_pair(q2, k2, v2, ok, first, first2):
    zero = jnp.zeros((), BF16)
    one = jnp.ones((), BF16)
    res, mx = [], []
    for sub in range(2):
        mine = first if sub == 0 else jnp.logical_not(first)
        mine2 = first2 if sub == 0 else jnp.logical_not(first2)
        s = lax.dot_general(jnp.where(mine, q2, zero), k2, _NT, preferred_element_type=F32)
        s = jnp.where(ok, s, NEG_INF)
        m = jnp.max(s, axis=-1, keepdims=True)
        p = jnp.exp2(s - m).astype(BF16)
        res.append(jnp.dot(p, jnp.where(mine2, v2, one), preferred_element_type=F32))
        mx.append(m)
    num = jnp.where(first, res[0], res[1])
    den_sw = jnp.where(first, res[1], res[0])
    den = pltpu.roll(den_sw, HEAD_DIM, 1)
    return num / den, jnp.where(first, mx[0], mx[1]) + jnp.log2(den)


def _dilated_kernel(q_ref, kc_ref, kp_ref, vc_ref, vp_ref, o_ref, o_scr, l_scr):
    win = q_ref.shape[0]
    t = Q_BLOCK
    qi = lax.broadcasted_iota(jnp.int32, (t, t), 0)
    kj = lax.broadcasted_iota(jnp.int32, (t, t), 1)
    ok_inner = jnp.concatenate([kj >= qi, kj <= qi], axis=1)
    ok_first = jnp.concatenate([(kj >= qi) & (pl.program_id(0) > 0), kj <= qi], axis=1)
    first = lax.broadcasted_iota(jnp.int32, (t, LANES), 1) < HEAD_DIM
    first2 = lax.broadcasted_iota(jnp.int32, (2 * t, LANES), 1) < HEAD_DIM

    for p, (_, dil) in enumerate(DIL_PATTERNS):
        span = dil * t
        n_blocks = win // span
        for r in range(dil):
            for cb in range(n_blocks):
                rows = pl.ds(r + cb * span, t, stride=dil) if dil > 1 else pl.ds(cb * t, t)
                if cb > 0:
                    prows = pl.ds(r + (cb - 1) * span, t, stride=dil) if dil > 1 else pl.ds((cb - 1) * t, t)
                    k_prev, v_prev, ok = kc_ref[prows, :], vc_ref[prows, :], ok_inner
                else:
                    prows = (pl.ds(r + (n_blocks - 1) * span, t, stride=dil) if dil > 1
                             else pl.ds((n_blocks - 1) * t, t))
                    k_prev, v_prev, ok = kp_ref[prows, :], vp_ref[prows, :], ok_first
                k2 = jnp.concatenate([k_prev, kc_ref[rows, :]], axis=0).astype(BF16)
                v2 = jnp.concatenate([v_prev, vc_ref[rows, :]], axis=0).astype(BF16)
                o2, lse2 = _band_head_pair(q_ref[rows, :].astype(BF16), k2, v2, ok, first, first2)
                o_scr[p, rows, :] = o2
                l_scr[p, rows, :] = lse2

    l1, l2, l3 = l_scr[0], l_scr[1], l_scr[2]
    m = jnp.maximum(jnp.maximum(l1, l2), l3)
    w1, w2, w3 = jnp.exp2(l1 - m), jnp.exp2(l2 - m), jnp.exp2(l3 - m)
    merged = (w1 * o_scr[0] + w2 * o_scr[1] + w3 * o_scr[2]) / (w1 + w2 + w3)
    o_ref[...] = merged.astype(o_ref.dtype)


def _dilated(q, k, v):
    s = q.shape[0]
    win = max(dil for _, dil in DIL_PATTERNS) * Q_BLOCK
    assert s % win == 0
    cur = pl.BlockSpec((win, LANES), lambda c, j: (c, j))
    prev = pl.BlockSpec((win, LANES), lambda c, j: (jnp.maximum(c - 1, 0), j))
    return pl.pallas_call(
        _dilated_kernel,
        grid=(s // win, D_MODEL // LANES),
        in_specs=[cur, cur, prev, cur, prev],
        out_specs=cur,
        out_shape=jax.ShapeDtypeStruct((s, D_MODEL), BF16),
        scratch_shapes=[pltpu.VMEM((len(DIL_PATTERNS), win, LANES), F32),
                        pltpu.VMEM((len(DIL_PATTERNS), win, LANES), F32)],
        compiler_params=_params(2),
        name="dilated_attention",
    )(q, k, k, v, v)


def _gelu_tanh(x):
    return 0.5 * x * (1.0 + jnp.tanh(np.sqrt(2.0 / np.pi).astype(np.float32) * (x + 0.044715 * (x * x * x))))


def _ffn_act_kernel(x_ref, halo_ref, g_ref, wg_ref, wu_ref, cw_ref, cb_ref, o_ref):
    i = pl.program_id(1)
    tm = x_ref.shape[0]
    h = _rmsnorm_rows(x_ref[...], g_ref[...]).astype(BF16)
    hh = _rmsnorm_rows(halo_ref[...], g_ref[...]).astype(BF16)
    wg = wg_ref[...]
    gate = jnp.dot(h, wg, preferred_element_type=F32)
    gate_halo = jnp.where(i > 0, jnp.dot(hh, wg, preferred_element_type=F32), 0.0)
    up = jnp.dot(h, wu_ref[...], preferred_element_type=F32)
    row = lax.broadcasted_iota(jnp.int32, gate.shape, 0)
    g_m1 = jnp.where(row == 0, gate_halo[7:8, :], pltpu.roll(gate, 1, 0))
    g_m2 = jnp.where(row == 0, gate_halo[6:7, :],
                     jnp.where(row == 1, gate_halo[7:8, :], pltpu.roll(gate, 2, 0)))
    cw = cw_ref[...]
    conv = cw[2:3, :] * gate + cw[1:2, :] * g_m1 + cw[0:1, :] * g_m2 + cb_ref[...]
    o_ref[...] = (_gelu_tanh(conv) * up).astype(o_ref.dtype)
    del tm


def _ffn_act(x, g, wg, wu, cw, cb, *, tm, tn):
    s, d = x.shape
    n = wg.shape[1]
    halo_rows = 8
    return pl.pallas_call(
        _ffn_act_kernel,
        grid=(n // tn, s // tm),
        in_specs=[pl.BlockSpec((tm, d), lambda j, i: (i, 0)),
                  pl.BlockSpec((halo_rows, d), lambda j, i: (jnp.maximum(i * (tm // halo_rows) - 1, 0), 0)),
                  pl.BlockSpec((1, d), lambda j, i: (0, 0)),
                  pl.BlockSpec((d, tn), lambda j, i: (0, j)),
                  pl.BlockSpec((d, tn), lambda j, i: (0, j)),
                  pl.BlockSpec((cw.shape[0], tn), lambda j, i: (0, j)),
                  pl.BlockSpec((1, tn), lambda j, i: (0, j))],
        out_specs=pl.BlockSpec((tm, tn), lambda j, i: (i, j)),
        out_shape=jax.ShapeDtypeStruct((s, n), BF16),
        compiler_params=_params(2),
        name="ffn_act",
    )(x, x, g, wg, wu, cw, cb)


def _ple_kernel(x_ref, p_ref, g_ref, wg_ref, wp_ref, o_ref):
    x = x_ref[...]
    z = jnp.dot(_rmsnorm_rows(x, g_ref[...]).astype(BF16), wg_ref[...], preferred_element_type=F32)
    gate = 1.0 / (1.0 + jnp.exp(-z))
    proj = jnp.dot(p_ref[...].astype(BF16), wp_ref[...], preferred_element_type=F32)
    o_ref[...] = x + gate * proj


def _ple(x, p, g, wg, wp, *, tm):
    s, d = x.shape
    return pl.pallas_call(
        _ple_kernel,
        grid=(s // tm,),
        in_specs=[pl.BlockSpec((tm, d), lambda i: (i, 0)),
                  pl.BlockSpec((tm, p.shape[1]), lambda i: (i, 0)),
                  pl.BlockSpec((1, d), lambda i: (0, 0)),
                  pl.BlockSpec(wg.shape, lambda i: (0, 0)),
                  pl.BlockSpec(wp.shape, lambda i: (0, 0))],
        out_specs=pl.BlockSpec((tm, d), lambda i: (i, 0)),
        out_shape=jax.ShapeDtypeStruct((s, d), F32),
        compiler_params=_params(1),
        name="ple",
    )(x, p, g, wg, wp)


def _rope_tables(s):
    half = HEAD_DIM // 2
    inv_freq = ROPE_THETA ** (-jnp.arange(half, dtype=F32) / half)
    ang = jnp.arange(s).astype(F32)[:, None] * inv_freq[None, :]
    cos, sin = jnp.cos(ang), jnp.sin(ang)
    return jnp.tile(cos, (1, 4)), jnp.tile(jnp.concatenate([-sin, sin], axis=1), (1, 2))


def _logit_bound(gq, gk):
    return (1.02 * HEAD_DIM * HEAD_DIM ** -0.5 * LOG2E) * jnp.max(jnp.abs(gq)) * jnp.max(jnp.abs(gk))


def _gain_tile(g, scale=1.0):
    return (jnp.tile(g.astype(F32), 2) * scale)[None, :]


def _even_w_in(w):
    hd = N_HEADS_A * HEAD_DIM
    cuts = np.cumsum([hd, hd, hd, N_HEADS_A, hd, hd, hd, IDX_HEADS * IDX_DIM, IDX_DIM, IDX_HEADS])[:-1]
    qa, ka, va, fa, qb, kb, vb, qi, ki, wi = jnp.split(w, [int(c) for c in cuts], axis=1)
    pad = jnp.zeros((w.shape[0], LANES - IDX_DIM - N_HEADS_A - IDX_HEADS), w.dtype)
    return jnp.concatenate([qa, ka, va, qb, kb, vb, qi, ki, fa, wi, pad], axis=1).astype(BF16)


def kernel(x, p, norm_mix, w_in_even, b_forget, q_norm_a, k_norm_a, q_norm_b, k_norm_b, w_in_odd, q_norm_c,
           k_norm_c, w_out, norm_ffn, w_gate, w_up, conv_w, conv_b, w_down, norm_ple, w_ple_gate, w_ple_proj):
    batch, s, d = x.shape
    assert batch == 1 and d == D_MODEL
    depth = norm_mix.shape[0]
    topk = min(TOPK_MAX, s // 4)
    t_attn = min(512, s)
    tm = min(512, s)
    tm_prep = min(256, s)
    qscale = HEAD_DIM ** -0.5 * LOG2E

    cos, sin = _rope_tables(s)
    lane = np.arange(LANES)
    bmat = jnp.asarray((lane[:, None] // HEAD_DIM == lane[None, :] // HEAD_DIM) / HEAD_DIM, BF16)
    tri = jnp.asarray(np.tril(np.ones((tm_prep, tm_prep))), BF16)

    xs = x[0]
    for i in range(depth):
        j = i // 2
        g_mix = norm_mix[i][None, :]
        w_o = w_out[i].astype(BF16)
        if i % 2 == 0:
            proj = _rms_matmul(xs, g_mix, _even_w_in(w_in_even[j]), tm=tm, tn=EV_WIDTH // 3)
            bf_pad = jnp.zeros((1, LANES), F32).at[0, MISC_F:MISC_F + N_HEADS_A].set(b_forget[j])
            bound_a = _logit_bound(q_norm_a[j], k_norm_a[j])
            bound_b = _logit_bound(q_norm_b[j], k_norm_b[j])
            shifts = jnp.stack([jnp.full((LANES,), LOGIT_CEIL_LOG2 - bound_a, F32),
                                jnp.full((LANES,), LOGIT_CEIL_LOG2 - bound_b, F32)])
            qa, ka, va_t, qb, kb, vb_t, qi, ki, wi_t = _prep_even(
                proj, cos, sin, _gain_tile(q_norm_a[j], qscale), _gain_tile(k_norm_a[j]),
                _gain_tile(q_norm_b[j], qscale), _gain_tile(k_norm_b[j]), bf_pad, shifts, bmat, tri,
                tm=tm_prep)
            out_a = _attention_any(qa, ka, va_t, None, bound_a, t=t_attn)
            mask = _indexer(qi, ki, wi_t, tq=LANES, tk=t_attn, topk=topk, group=16)
            out_b = _attention_any(qb, kb, vb_t, mask, bound_b, t=t_attn)
            half = N_HEADS_A * HEAD_DIM
            xs = _matmul_res([(out_a, w_o[:half]), (out_b, w_o[half:])], xs, tm=tm)
        else:
            proj = _rms_matmul(xs, g_mix, w_in_odd[j].astype(BF16), tm=tm, tn=D_MODEL)
            q, k, v = _prep_odd(proj, cos, sin, _gain_tile(q_norm_c[j], qscale), _gain_tile(k_norm_c[j]),
                                bmat, tm=tm_prep)
            xs = _matmul_res([(_dilated(q, k, v), w_o)], xs, tm=tm)
        act = _ffn_act(xs, norm_ffn[i][None, :], w_gate[i].astype(BF16), w_up[i].astype(BF16),
                       conv_w[i], conv_b[i][None, :], tm=tm_prep, tn=D_FF // 2)
        xs = _matmul_res([(act, w_down[i].astype(BF16))], xs, tm=tm)
        xs = _ple(xs, p[i, 0], norm_ple[i][None, :], w_ple_gate[i].astype(BF16),
                  w_ple_proj[i].astype(BF16), tm=tm)
    return xs[None]
```

```python
import functools

import numpy as np
import jax
import jax.numpy as jnp
from jax import lax
from jax.experimental import pallas as pl
from jax.experimental.pallas import tpu as pltpu

F32 = jnp.float32
BF16 = jnp.bfloat16

D_MODEL = 1024
HEAD_DIM = 64
N_HEADS = D_MODEL // HEAD_DIM
N_HEADS_A = N_HEADS // 2
N_HEADS_B = N_HEADS - N_HEADS_A
IDX_HEADS = 16
IDX_DIM = 64
TOPK_MAX = 256
DIL_PATTERNS = ((128, 1), (512, 4), (2048, 16))
Q_BLOCK = 128
D_FF = 2816
P_DIM = 256
ROPE_THETA = 10000.0
EPS = 1e-6
DEPTH = 4

LANES = 128
LOG2E = 1.4426950408889634
NEG_INF = float("-inf")
INT_MIN = -(2 ** 31)
VMEM_LIMIT_BYTES = 56 * 1024 * 1024
LOGIT_CEIL_LOG2 = 40.0
LOGIT_BOUND_MAX = 80.0
LOGIT_DEAD = LOGIT_CEIL_LOG2 + 160.0

EV_QA, EV_KA, EV_VA, EV_QB, EV_KB, EV_VB, EV_QI, EV_MISC = 0, 512, 1024, 1536, 2048, 2560, 3072, 4096
EV_WIDTH = 4224
MISC_F = 64
MISC_W = 72

_NT = (((1,), (1,)), ((), ()))


def _params(n_axes):
    return pltpu.CompilerParams(dimension_semantics=("arbitrary",) * n_axes,
                                vmem_limit_bytes=VMEM_LIMIT_BYTES)


def _rmsnorm_rows(x, g):
    ms = jnp.mean(x * x, axis=-1, keepdims=True)
    return x * lax.rsqrt(ms + EPS) * g


def _split3(x):
    hi = x.astype(BF16)
    r = x - hi.astype(F32)
    mid = r.astype(BF16)
    lo = (r - mid.astype(F32)).astype(BF16)
    return hi, mid, lo


def _rms_matmul_kernel(x_ref, g_ref, w_ref, o_ref):
    h = _rmsnorm_rows(x_ref[...], g_ref[...]).astype(BF16)
    o_ref[...] = jnp.dot(h, w_ref[...], preferred_element_type=F32)


def _rms_matmul(x, g, w, *, tm, tn):
    s, d = x.shape
    n = w.shape[1]
    return pl.pallas_call(
        _rms_matmul_kernel,
        grid=(n // tn, s // tm),
        in_specs=[pl.BlockSpec((tm, d), lambda j, i: (i, 0)),
                  pl.BlockSpec((1, d), lambda j, i: (0, 0)),
                  pl.BlockSpec((d, tn), lambda j, i: (0, j))],
        out_specs=pl.BlockSpec((tm, tn), lambda j, i: (i, j)),
        out_shape=jax.ShapeDtypeStruct((s, n), F32),
        compiler_params=_params(2),
        name="rms_matmul",
    )(x, g, w)


def _head_rmsnorm(x, gain, bmat):
    sq = x * x
    hi = sq.astype(BF16)
    lo = (sq - hi.astype(F32)).astype(BF16)
    ms = (jnp.dot(hi, bmat, preferred_element_type=F32)
          + jnp.dot(lo, bmat, preferred_element_type=F32))
    return x * lax.rsqrt(ms + EPS) * gain


def _rope(x, cos, sin_signed, low_half):
    rot = jnp.where(low_half, pltpu.roll(x, LANES - 32, 1), pltpu.roll(x, 32, 1))
    return x * cos + rot * sin_signed


def _log_sigmoid(z):
    return jnp.minimum(z, 0.0) - jnp.log1p(jnp.exp(-jnp.abs(z)))


def _prep_even_kernel(proj_ref, cos_ref, sin_ref, gqa_ref, gka_ref, gqb_ref, gkb_ref, bf_ref,
                      shift_ref, bmat_ref, tri_ref,
                      qa_ref, ka_ref, va_ref, qb_ref, kb_ref, vb_ref, qi_ref, ki_ref, wi_ref, cs_ref, carry):
    tm = proj_ref.shape[0]
    shift_a = shift_ref[0:1, :]
    shift_b = shift_ref[1:2, :]

    @pl.when(pl.program_id(0) == 0)
    def _():
        carry[...] = jnp.zeros_like(carry)

    cos = cos_ref[...]
    sin = sin_ref[...]
    bmat = bmat_ref[...]
    lane = lax.broadcasted_iota(jnp.int32, (tm, LANES), 1)
    low_half = (lane & 32) == 0
    head0 = lane < HEAD_DIM

    misc = proj_ref[:, EV_MISC:EV_MISC + LANES]

    logf = _log_sigmoid(misc + bf_ref[...])
    logf = jnp.where((lane >= MISC_F) & (lane < MISC_F + N_HEADS_A), logf, 0.0)
    tri = tri_ref[...]
    f_hi, f_mid, f_lo = _split3(logf)
    c = (jnp.dot(tri, f_hi, preferred_element_type=F32)
         + jnp.dot(tri, f_mid, preferred_element_type=F32)
         + jnp.dot(tri, f_lo, preferred_element_type=F32)) + carry[0:1, :]
    carry[0:1, :] = c[tm - 1:tm, :]
    cs_ref[...] = c * LOG2E
    c_hi, c_mid, c_lo = (t.astype(F32) for t in _split3(c * LOG2E))

    def place(block, h):
        return block if h % 2 == 0 else pltpu.roll(block, HEAD_DIM, 1)

    for hp in range(N_HEADS_A // 2):
        cols = slice(hp * LANES, (hp + 1) * LANES)
        qa = _head_rmsnorm(proj_ref[:, EV_QA + hp * LANES:EV_QA + (hp + 1) * LANES], gqa_ref[...], bmat)
        ka = _head_rmsnorm(proj_ref[:, EV_KA + hp * LANES:EV_KA + (hp + 1) * LANES], gka_ref[...], bmat)
        va = proj_ref[:, EV_VA + hp * LANES:EV_VA + (hp + 1) * LANES]
        qb = _rope(_head_rmsnorm(proj_ref[:, EV_QB + hp * LANES:EV_QB + (hp + 1) * LANES], gqb_ref[...], bmat),
                   cos, sin, low_half)
        kb = _rope(_head_rmsnorm(proj_ref[:, EV_KB + hp * LANES:EV_KB + (hp + 1) * LANES], gkb_ref[...], bmat),
                   cos, sin, low_half)
        vb = proj_ref[:, EV_VB + hp * LANES:EV_VB + (hp + 1) * LANES]
        del cols
        for sub in range(2):
            h = 2 * hp + sub
            src = MISC_F + h
            q_aug = jnp.where(
                head0, place(qa, h),
                jnp.where(lane == 64, pltpu.roll(c_hi, (64 - src) % LANES, 1),
                          jnp.where(lane == 65, pltpu.roll(c_mid, (65 - src) % LANES, 1),
                                    jnp.where(lane == 66, pltpu.roll(c_lo, (66 - src) % LANES, 1),
                                              jnp.where(lane < 70, 1.0,
                                                        jnp.where(lane == 70, shift_a, 0.0))))))
            k_aug = jnp.where(
                head0, place(ka, h),
                jnp.where(lane < 67, 1.0,
                          jnp.where(lane == 67, -pltpu.roll(c_hi, (67 - src) % LANES, 1),
                                    jnp.where(lane == 68, -pltpu.roll(c_mid, (68 - src) % LANES, 1),
                                              jnp.where(lane == 69, -pltpu.roll(c_lo, (69 - src) % LANES, 1),
                                                        jnp.where(lane == 70, 1.0, 0.0))))))
            ones_col = jnp.where(lane == HEAD_DIM, 1.0, 0.0)
            qa_ref[h] = q_aug.astype(BF16)
            ka_ref[h] = k_aug.astype(BF16)
            va_ref[h] = jnp.where(head0, place(va, h), ones_col).T.astype(BF16)
            qb_ref[h] = jnp.where(head0, place(qb, h), jnp.where(lane == HEAD_DIM, shift_b, 0.0)).astype(BF16)
            kb_ref[h] = jnp.where(head0, place(kb, h), ones_col).astype(BF16)
            vb_ref[h] = jnp.where(head0, place(vb, h), ones_col).T.astype(BF16)

    for blk in range(IDX_HEADS * IDX_DIM // LANES):
        cols = slice(EV_QI + blk * LANES, EV_QI + (blk + 1) * LANES)
        qi = _rope(proj_ref[:, cols], cos, sin, low_half) * (IDX_DIM ** -0.5)
        qi_ref[2 * blk] = qi[:, 0:IDX_DIM].astype(BF16)
        qi_ref[2 * blk + 1] = qi[:, IDX_DIM:2 * IDX_DIM].astype(BF16)
    ki_ref[...] = _rope(misc, cos, sin, low_half)[:, 0:IDX_DIM].astype(BF16)
    wi_ref[...] = (misc * (IDX_HEADS ** -0.5)).T


def _prep_even(proj, cos, sin, gqa, gka, gqb, gkb, bf_pad, shifts, bmat, tri, *, tm):
    s = proj.shape[0]
    row = lambda w: pl.BlockSpec((tm, w), lambda i: (i, 0))
    const = lambda a: pl.BlockSpec(a.shape, lambda i: (0, 0))
    heads = pl.BlockSpec((N_HEADS_A, tm, LANES), lambda i: (0, i, 0))
    heads_shape = jax.ShapeDtypeStruct((N_HEADS_A, s, LANES), BF16)
    heads_t = pl.BlockSpec((N_HEADS_A, LANES, tm), lambda i: (0, 0, i))
    heads_t_shape = jax.ShapeDtypeStruct((N_HEADS_A, LANES, s), BF16)
    return pl.pallas_call(
        _prep_even_kernel,
        grid=(s // tm,),
        in_specs=[row(EV_WIDTH), row(LANES), row(LANES), const(gqa), const(gka), const(gqb), const(gkb),
                  const(bf_pad), const(shifts), const(bmat), const(tri)],
        out_specs=[heads, heads, heads_t, heads, heads, heads_t,
                   pl.BlockSpec((IDX_HEADS, tm, IDX_DIM), lambda i: (0, i, 0)),
                   row(IDX_DIM),
                   pl.BlockSpec((LANES, tm), lambda i: (0, i)),
                   row(LANES)],
        out_shape=[heads_shape, heads_shape, heads_t_shape, heads_shape, heads_shape, heads_t_shape,
                   jax.ShapeDtypeStruct((IDX_HEADS, s, IDX_DIM), BF16),
                   jax.ShapeDtypeStruct((s, IDX_DIM), BF16),
                   jax.ShapeDtypeStruct((LANES, s), F32),
                   jax.ShapeDtypeStruct((s, LANES), F32)],
        scratch_shapes=[pltpu.VMEM((8, LANES), F32)],
        compiler_params=_params(1),
        name="prep_even",
    )(proj, cos, sin, gqa, gka, gqb, gkb, bf_pad, shifts, bmat, tri)


def _float_key(x):
    bits = pltpu.bitcast(x, jnp.int32)
    return bits ^ ((bits >> 31) & 0x7FFFFFFF)


def _key_float(k):
    return pltpu.bitcast(k ^ ((k >> 31) & 0x7FFFFFFF), F32)


def _kth_search(count_fn, lo_k, hi_k, c_lo, done, kf, *, fixed_steps=None, steps_per_check=1,
                stop_on_count=True):
    def finished(lo, hi, c, d):
        d = d | (lo + 1 >= hi)
        return jnp.where(d | (c == kf) if stop_on_count else d, 1.0, 0.0)

    def halve(lo, hi, c, d):
        live = d == 0.0
        cand = (lo >> 1) + (hi >> 1) + (lo & hi & 1)
        n = count_fn(_key_float(cand))
        take = (n >= kf) & live
        drop = (n < kf) & live
        lo = jnp.where(take, cand, lo)
        c = jnp.where(take, n, c)
        hi = jnp.where(drop, cand, hi)
        return lo, hi, c, finished(lo, hi, c, d > 0.0)

    d0 = finished(lo_k, hi_k, c_lo, done)
    if fixed_steps is not None:
        lo_k, _, c_lo, _ = lax.fori_loop(0, fixed_steps, lambda _, st: halve(*st), (lo_k, hi_k, c_lo, d0))
        return lo_k, c_lo

    def body(state):
        st = state[:4]
        for _ in range(steps_per_check):
            st = halve(*st)
        return (*st, jnp.max(1.0 - st[3]))

    lo_k, _, c_lo, _, _ = lax.while_loop(lambda state: state[4] > 0.0, body,
                                         (lo_k, hi_k, c_lo, d0, jnp.max(1.0 - d0)))
    return lo_k, c_lo


def _indexer_kernel(qi_ref, ki_ref, wi_ref, bias_ref, sc_scr, g_scr, *, tq, tk, topk, group):
    i = pl.program_id(0)
    s_keys = sc_scr.shape[0]
    n_chunks = s_keys // tk
    nk = ((i + 1) * tq + tk - 1) // tk
    sub = 8
    g_rows = tk // group
    kf = float(topk)
    lowest = float(np.finfo(np.float32).min)
    qpos = lax.broadcasted_iota(jnp.int32, (1, tq), 1) + i * tq
    w = wi_ref[...]

    g_scr[...] = jnp.full(g_scr.shape, NEG_INF, F32)

    def score_chunk(c, gmax):
        ks = pl.multiple_of(c * tk, tk)
        ki = ki_ref[pl.ds(ks, tk), :]
        acc = jnp.zeros((tk, tq), F32)
        for hp in range(IDX_HEADS // 2):
            q2 = jnp.concatenate([qi_ref[2 * hp], qi_ref[2 * hp + 1]], axis=0)
            l01 = lax.dot_general(ki, q2, _NT, preferred_element_type=F32)
            w0 = w[MISC_W + 2 * hp:MISC_W + 2 * hp + 1, :]
            w1 = w[MISC_W + 2 * hp + 1:MISC_W + 2 * hp + 2, :]
            acc = acc + w0 * jnp.maximum(l01[:, 0:tq], 0.0) + w1 * jnp.maximum(l01[:, tq:2 * tq], 0.0)
        kpos = lax.broadcasted_iota(jnp.int32, (tk, tq), 0) + c * tk
        sc = jnp.where(kpos <= qpos, acc, NEG_INF)
        sc_scr[pl.ds(ks, tk), :] = sc
        g = jnp.max(sc.reshape(tk // (group * sub), group, sub, tq), axis=1).reshape(g_rows, tq)
        g_scr[pl.ds(pl.multiple_of(c * g_rows, g_rows), g_rows), :] = g
        return jnp.maximum(gmax, jnp.max(g, axis=0, keepdims=True))

    gmax = lax.fori_loop(0, nk, score_chunk, jnp.full((1, tq), NEG_INF, F32))

    def count_rows(ref, rows_per_step, n_steps, cand):
        ways = min(8, rows_per_step // sub)

        def step(c, cnt):
            r0 = pl.multiple_of(c * rows_per_step, rows_per_step)
            ones = jnp.where(ref[pl.ds(r0, rows_per_step), :] >= cand, 1.0, 0.0)
            return cnt + jnp.sum(ones.reshape(rows_per_step // (ways * sub), ways, sub, tq), axis=0)

        cnt = lax.fori_loop(0, n_steps, step, jnp.zeros((ways, sub, tq), F32))
        return jnp.sum(jnp.sum(cnt, axis=0), axis=0, keepdims=True)

    g_piece = min(256, g_scr.shape[0])
    count_groups = lambda cand: count_rows(g_scr, g_piece, (nk * g_rows + g_piece - 1) // g_piece, cand)
    count_all = lambda cand: count_rows(sc_scr, tk, nk, cand)

    lo0 = _float_key(jnp.full((1, tq), lowest, F32))
    hi0 = _float_key(gmax) + 1
    few = qpos < topk
    c_groups = count_groups(jnp.full((1, tq), lowest, F32))
    u_k, _ = _kth_search(count_groups, lo0, hi0, c_groups, few | (c_groups < kf), kf, fixed_steps=32)
    u = _key_float(u_k)
    c_u = count_all(u)
    n_max = count_groups(u)
    rank = kf - (c_u - n_max)
    has_y = jnp.logical_not(few) & (rank >= 1.0) & (rank <= n_max)
    y_k, _ = _kth_search(count_groups, u_k, hi0, n_max, jnp.logical_not(has_y), rank, fixed_steps=32,
                         stop_on_count=False)
    hi1 = jnp.where(has_y, y_k + 1, hi0)
    t_k, _ = _kth_search(count_all, u_k, hi1, c_u, few, kf, steps_per_check=2)
    thr = jnp.where(few, lowest, _key_float(t_k))

    def bias_chunk(c, carry):
        ks = pl.multiple_of(c * tk, tk)
        bias_ref[pl.ds(ks, tk), :] = jnp.where(sc_scr[pl.ds(ks, tk), :] >= thr, 1.0, 0.0).astype(BF16)
        return carry

    lax.fori_loop(0, nk, bias_chunk, 0)

    def fill_chunk(c, carry):
        bias_ref[pl.ds(pl.multiple_of(c * tk, tk), tk), :] = jnp.zeros((tk, tq), BF16)
        return carry

    lax.fori_loop(nk, n_chunks, fill_chunk, 0)


def _indexer(qi, ki, wi_t, *, tq, tk, topk, group):
    s = ki.shape[0]
    return pl.pallas_call(
        functools.partial(_indexer_kernel, tq=tq, tk=tk, topk=topk, group=group),
        grid=(s // tq,),
        in_specs=[pl.BlockSpec((IDX_HEADS, tq, IDX_DIM), lambda i: (0, i, 0)),
                  pl.BlockSpec((s, IDX_DIM), lambda i: (0, 0)),
                  pl.BlockSpec((LANES, tq), lambda i: (0, i))],
        out_specs=pl.BlockSpec((None, s, tq), lambda i: (i, 0, 0)),
        out_shape=jax.ShapeDtypeStruct((s // tq, s, tq), BF16),
        scratch_shapes=[pltpu.VMEM((s, tq), F32), pltpu.VMEM((s // group, tq), F32)],
        compiler_params=_params(1),
        name="dsa_indexer",
    )(qi, ki, wi_t)


def _attn_kernel(*refs, has_mask, bounded):
    first_ref, q_ref, k_ref, vt_ref = refs[:4]
    refs = refs[1:]
    mask_ref = refs[3] if has_mask else None
    o_ref = refs[3 + has_mask]
    scratch = refs[4 + has_mask:]
    acc_scr = scratch[-1]
    m_scr = None if bounded else scratch[0]
    n_heads, t, _ = q_ref.shape
    i = pl.program_id(0)
    j = pl.program_id(1)

    @pl.when(j == 0)
    def _():
        acc_scr[...] = jnp.zeros(acc_scr.shape, F32)
        if not bounded:
            m_scr[...] = jnp.full(m_scr.shape, NEG_INF, F32)

    def step(diagonal):
        if has_mask:
            mask = jnp.concatenate([mask_ref[b] for b in range(mask_ref.shape[0])], axis=1)
            if not bounded:
                selected = mask.astype(F32) > 0.0
        if diagonal:
            causal = (lax.broadcasted_iota(jnp.int32, (t, t), 0)
                      <= lax.broadcasted_iota(jnp.int32, (t, t), 1))

        def logits(h):
            return lax.dot_general(k_ref[h], q_ref[h], _NT, preferred_element_type=F32)

        s_next = logits(0)
        for h in range(n_heads):
            s = s_next
            if h + 1 < n_heads:
                s_next = logits(h + 1)
            if diagonal:
                s = jnp.where(causal, s, NEG_INF)
            if bounded:
                p = jnp.exp2(s).astype(BF16)
                if has_mask:
                    p = p * mask
                acc_scr[h] = acc_scr[h] + jnp.dot(vt_ref[h], p, preferred_element_type=F32)
            else:
                if has_mask:
                    s = jnp.where(selected, s, NEG_INF)
                m_old = m_scr[h][0:1, :]
                m_new = jnp.maximum(m_old, jnp.max(s, axis=0, keepdims=True))
                m_safe = jnp.where(m_new == NEG_INF, 0.0, m_new)
                p = jnp.exp2(s - m_safe)
                alpha = jnp.exp2(m_old - m_safe)
                acc_scr[h] = alpha * acc_scr[h] + jnp.dot(vt_ref[h], p.astype(BF16), preferred_element_type=F32)
                m_scr[h] = jnp.broadcast_to(m_new, m_scr.shape[1:])

    @pl.when((j < i) & (j >= first_ref[i]))
    def _():
        step(False)

    @pl.when(j == i)
    def _():
        step(True)
        for hp in range(n_heads // 2):
            halves = []
            for h in (2 * hp, 2 * hp + 1):
                a = acc_scr[h]
                halves.append(a[0:HEAD_DIM, :] / a[HEAD_DIM:HEAD_DIM + 1, :])
            o_ref[:, hp * LANES:(hp + 1) * LANES] = jnp.concatenate(halves, axis=0).T.astype(o_ref.dtype)


def _attention(q, k, vt, mask, first, *, t, bounded):
    n_heads, s, _ = q.shape
    sub = 8
    key_tile = lambda i, j, first_ref: jnp.clip(j, first_ref[i], i)
    in_specs = [pl.BlockSpec((n_heads, t, LANES), lambda i, j, f: (0, i, 0)),
                pl.BlockSpec((n_heads, t, LANES), lambda i, j, f: (0, key_tile(i, j, f), 0)),
                pl.BlockSpec((n_heads, LANES, t), lambda i, j, f: (0, 0, key_tile(i, j, f)))]
    args = [q, k, vt]
    if mask is not None:
        in_specs.append(pl.BlockSpec((t // LANES, t, LANES), lambda i, j, f: (i, key_tile(i, j, f), 0)))
        args.append(mask)
    scratch = [pltpu.VMEM((n_heads, LANES, t), F32)]
    if not bounded:
        scratch.insert(0, pltpu.VMEM((n_heads, sub, t), F32))
    name = ("dsa_attention" if mask is not None else "fox_attention") + ("" if bounded else "_online")
    return pl.pallas_call(
        functools.partial(_attn_kernel, has_mask=mask is not None, bounded=bounded),
        grid_spec=pltpu.PrefetchScalarGridSpec(
            num_scalar_prefetch=1,
            grid=(s // t, s // t),
            in_specs=in_specs,
            out_specs=pl.BlockSpec((t, n_heads * HEAD_DIM), lambda i, j, f: (i, 0)),
            scratch_shapes=scratch),
        out_shape=jax.ShapeDtypeStruct((s, n_heads * HEAD_DIM), BF16),
        compiler_params=_params(2),
        name=name,
    )(first, *args)


def _attention_any(q, k, vt, mask, logit_bound, first, *, t):
    operands = (q, k, vt, first) if mask is None else (q, k, vt, first, mask)
    m = lambda ops: ops[4] if mask is not None else None
    return lax.cond(logit_bound <= LOGIT_BOUND_MAX,
                    lambda *ops: _attention(ops[0], ops[1], ops[2], m(ops), ops[3], t=t, bounded=True),
                    lambda *ops: _attention(ops[0], ops[1], ops[2], m(ops), jnp.zeros_like(ops[3]), t=t,
                                            bounded=False),
                    *operands)


def _first_live_tile(cs, t):
    top = cs[0::t]
    bottom = cs[t - 1::t]
    live = jnp.any(top[:, None, :] - bottom[None, :, :] > -LOGIT_DEAD, axis=-1)
    return jnp.argmax(live, axis=1).astype(jnp.int32)


def _matmul_res_kernel(*refs, n_pairs):
    res_ref = refs[2 * n_pairs]
    o_ref = refs[2 * n_pairs + 1]
    acc = res_ref[...]
    for p in range(n_pairs):
        acc = acc + jnp.dot(refs[2 * p][...], refs[2 * p + 1][...], preferred_element_type=F32)
    o_ref[...] = acc


def _matmul_res(pairs, res, *, tm):
    s, n = res.shape
    in_specs, args = [], []
    for a, w in pairs:
        in_specs += [pl.BlockSpec((tm, a.shape[1]), lambda i: (i, 0)),
                     pl.BlockSpec(w.shape, lambda i: (0, 0))]
        args += [a, w]
    in_specs.append(pl.BlockSpec((tm, n), lambda i: (i, 0)))
    args.append(res)
    return pl.pallas_call(
        functools.partial(_matmul_res_kernel, n_pairs=len(pairs)),
        grid=(s // tm,),
        in_specs=in_specs,
        out_specs=pl.BlockSpec((tm, n), lambda i: (i, 0)),
        out_shape=jax.ShapeDtypeStruct((s, n), F32),
        compiler_params=_params(1),
        name="matmul_residual",
    )(*args)


def _prep_odd_kernel(proj_ref, cos_ref, sin_ref, gq_ref, gk_ref, bmat_ref, q_ref, k_ref, v_ref):
    tm = proj_ref.shape[0]
    cos = cos_ref[...]
    sin = sin_ref[...]
    bmat = bmat_ref[...]
    lane = lax.broadcasted_iota(jnp.int32, (tm, LANES), 1)
    low_half = (lane & 32) == 0
    for blk in range(D_MODEL // LANES):
        cols = slice(blk * LANES, (blk + 1) * LANES)
        q = proj_ref[:, blk * LANES:(blk + 1) * LANES]
        k = proj_ref[:, D_MODEL + blk * LANES:D_MODEL + (blk + 1) * LANES]
        q_ref[:, cols] = _rope(_head_rmsnorm(q, gq_ref[...], bmat), cos, sin, low_half)
        k_ref[:, cols] = _rope(_head_rmsnorm(k, gk_ref[...], bmat), cos, sin, low_half)
        v_ref[:, cols] = proj_ref[:, 2 * D_MODEL + blk * LANES:2 * D_MODEL + (blk + 1) * LANES]


def _prep_odd(proj, cos, sin, gq, gk, bmat, *, tm):
    s = proj.shape[0]
    row = lambda w: pl.BlockSpec((tm, w), lambda i: (i, 0))
    const = lambda a: pl.BlockSpec(a.shape, lambda i: (0, 0))
    shape = jax.ShapeDtypeStruct((s, D_MODEL), F32)
    return pl.pallas_call(
        _prep_odd_kernel,
        grid=(s // tm,),
        in_specs=[row(3 * D_MODEL), row(LANES), row(LANES), const(gq), const(gk), const(bmat)],
        out_specs=[row(D_MODEL)] * 3,
        out_shape=[shape] * 3,
        compiler_params=_params(1),
        name="prep_odd",
    )(proj, cos, sin, gq, gk, bmat)


def _band_head_pair(q2, k2, v2, ok, first, first2):
    zero = jnp.zeros((), BF16)
    one = jnp.ones((), BF16)
    res, mx = [], []
    for sub in range(2):
        mine = first if sub == 0 else jnp.logical_not(first)
        mine2 = first2 if sub == 0 else jnp.logical_not(first2)
        s = lax.dot_general(jnp.where(mine, q2, zero), k2, _NT, preferred_element_type=F32)
        s = jnp.where(ok, s, NEG_INF)
        m = jnp.max(s, axis=-1, keepdims=True)
        p = jnp.exp2(s - m).astype(BF16)
        res.append(jnp.dot(p, jnp.where(mine2, v2, one), preferred_element_type=F32))
        mx.append(m)
    num = jnp.where(first, res[0], res[1])
    den_sw = jnp.where(first, res[1], res[0])
    den = pltpu.roll(den_sw, HEAD_DIM, 1)
    return num / den, jnp.where(first, mx[0], mx[1]) + jnp.log2(den)


def _dilated_kernel(q_ref, kc_ref, kp_ref, vc_ref, vp_ref, o_ref, o_scr, l_scr):
    win = q_ref.shape[0]
    t = Q_BLOCK
    qi = lax.broadcasted_iota(jnp.int32, (t, t), 0)
    kj = lax.broadcasted_iota(jnp.int32, (t, t), 1)
    ok_inner = jnp.concatenate([kj >= qi, kj <= qi], axis=1)
    ok_first = jnp.concatenate([(kj >= qi) & (pl.program_id(0) > 0), kj <= qi], axis=1)
    first = lax.broadcasted_iota(jnp.int32, (t, LANES), 1) < HEAD_DIM
    first2 = lax.broadcasted_iota(jnp.int32, (2 * t, LANES), 1) < HEAD_DIM

    for p, (_, dil) in enumerate(DIL_PATTERNS):
        span = dil * t
        n_blocks = win // span
        for r in range(dil):
            for cb in range(n_blocks):
                rows = pl.ds(r + cb * span, t, stride=dil) if dil > 1 else pl.ds(cb * t, t)
                if cb > 0:
                    prows = pl.ds(r + (cb - 1) * span, t, stride=dil) if dil > 1 else pl.ds((cb - 1) * t, t)
                    k_prev, v_prev, ok = kc_ref[prows, :], vc_ref[prows, :], ok_inner
                else:
                    prows = (pl.ds(r + (n_blocks - 1) * span, t, stride=dil) if dil > 1
                             else pl.ds((n_blocks - 1) * t, t))
                    k_prev, v_prev, ok = kp_ref[prows, :], vp_ref[prows, :], ok_first
                k2 = jnp.concatenate([k_prev, kc_ref[rows, :]], axis=0).astype(BF16)
                v2 = jnp.concatenate([v_prev, vc_ref[rows, :]], axis=0).astype(BF16)
                o2, lse2 = _band_head_pair(q_ref[rows, :].astype(BF16), k2, v2, ok, first, first2)
                o_scr[p, rows, :] = o2
                l_scr[p, rows, :] = lse2

    l1, l2, l3 = l_scr[0], l_scr[1], l_scr[2]
    m = jnp.maximum(jnp.maximum(l1, l2), l3)
    w1, w2, w3 = jnp.exp2(l1 - m), jnp.exp2(l2 - m), jnp.exp2(l3 - m)
    merged = (w1 * o_scr[0] + w2 * o_scr[1] + w3 * o_scr[2]) / (w1 + w2 + w3)
    o_ref[...] = merged.astype(o_ref.dtype)


def _dilated(q, k, v):
    s = q.shape[0]
    win = max(dil for _, dil in DIL_PATTERNS) * Q_BLOCK
    assert s % win == 0
    cur = pl.BlockSpec((win, LANES), lambda c, j: (c, j))
    prev = pl.BlockSpec((win, LANES), lambda c, j: (jnp.maximum(c - 1, 0), j))
    return pl.pallas_call(
        _dilated_kernel,
        grid=(s // win, D_MODEL // LANES),
        in_specs=[cur, cur, prev, cur, prev],
        out_specs=cur,
        out_shape=jax.ShapeDtypeStruct((s, D_MODEL), BF16),
        scratch_shapes=[pltpu.VMEM((len(DIL_PATTERNS), win, LANES), F32),
                        pltpu.VMEM((len(DIL_PATTERNS), win, LANES), F32)],
        compiler_params=_params(2),
        name="dilated_attention",
    )(q, k, k, v, v)


def _gelu_tanh(x):
    return 0.5 * x * (1.0 + jnp.tanh(np.sqrt(2.0 / np.pi).astype(np.float32) * (x + 0.044715 * (x * x * x))))


def _ffn_act_kernel(x_ref, halo_ref, g_ref, wg_ref, wu_ref, cw_ref, cb_ref, o_ref):
    i = pl.program_id(1)
    tm = x_ref.shape[0]
    h = _rmsnorm_rows(x_ref[...], g_ref[...]).astype(BF16)
    hh = _rmsnorm_rows(halo_ref[...], g_ref[...]).astype(BF16)
    wg = wg_ref[...]
    gate = jnp.dot(h, wg, preferred_element_type=F32)
    gate_halo = jnp.where(i > 0, jnp.dot(hh, wg, preferred_element_type=F32), 0.0)
    up = jnp.dot(h, wu_ref[...], preferred_element_type=F32)
    row = lax.broadcasted_iota(jnp.int32, gate.shape, 0)
    g_m1 = jnp.where(row == 0, gate_halo[7:8, :], pltpu.roll(gate, 1, 0))
    g_m2 = jnp.where(row == 0, gate_halo[6:7, :],
                     jnp.where(row == 1, gate_halo[7:8, :], pltpu.roll(gate, 2, 0)))
    cw = cw_ref[...]
    conv = cw[2:3, :] * gate + cw[1:2, :] * g_m1 + cw[0:1, :] * g_m2 + cb_ref[...]
    o_ref[...] = (_gelu_tanh(conv) * up).astype(o_ref.dtype)
    del tm


def _ffn_act(x, g, wg, wu, cw, cb, *, tm, tn):
    s, d = x.shape
    n = wg.shape[1]
    halo_rows = 8
    return pl.pallas_call(
        _ffn_act_kernel,
        grid=(n // tn, s // tm),
        in_specs=[pl.BlockSpec((tm, d), lambda j, i: (i, 0)),
                  pl.BlockSpec((halo_rows, d), lambda j, i: (jnp.maximum(i * (tm // halo_rows) - 1, 0), 0)),
                  pl.BlockSpec((1, d), lambda j, i: (0, 0)),
                  pl.BlockSpec((d, tn), lambda j, i: (0, j)),
                  pl.BlockSpec((d, tn), lambda j, i: (0, j)),
                  pl.BlockSpec((cw.shape[0], tn), lambda j, i: (0, j)),
                  pl.BlockSpec((1, tn), lambda j, i: (0, j))],
        out_specs=pl.BlockSpec((tm, tn), lambda j, i: (i, j)),
        out_shape=jax.ShapeDtypeStruct((s, n), BF16),
        compiler_params=_params(2),
        name="ffn_act",
    )(x, x, g, wg, wu, cw, cb)


def _ple_kernel(x_ref, p_ref, g_ref, wg_ref, wp_ref, o_ref):
    x = x_ref[...]
    z = jnp.dot(_rmsnorm_rows(x, g_ref[...]).astype(BF16), wg_ref[...], preferred_element_type=F32)
    gate = 1.0 / (1.0 + jnp.exp(-z))
    proj = jnp.dot(p_ref[...].astype(BF16), wp_ref[...], preferred_element_type=F32)
    o_ref[...] = x + gate * proj


def _ple(x, p, g, wg, wp, *, tm):
    s, d = x.shape
    return pl.pallas_call(
        _ple_kernel,
        grid=(s // tm,),
        in_specs=[pl.BlockSpec((tm, d), lambda i: (i, 0)),
                  pl.BlockSpec((tm, p.shape[1]), lambda i: (i, 0)),
                  pl.BlockSpec((1, d), lambda i: (0, 0)),
                  pl.BlockSpec(wg.shape, lambda i: (0, 0)),
                  pl.BlockSpec(wp.shape, lambda i: (0, 0))],
        out_specs=pl.BlockSpec((tm, d), lambda i: (i, 0)),
        out_shape=jax.ShapeDtypeStruct((s, d), F32),
        compiler_params=_params(1),
        name="ple",
    )(x, p, g, wg, wp)


def _rope_tables(s):
    half = HEAD_DIM // 2
    inv_freq = ROPE_THETA ** (-jnp.arange(half, dtype=F32) / half)
    ang = jnp.arange(s).astype(F32)[:, None] * inv_freq[None, :]
    cos, sin = jnp.cos(ang), jnp.sin(ang)
    return jnp.tile(cos, (1, 4)), jnp.tile(jnp.concatenate([-sin, sin], axis=1), (1, 2))


def _logit_bound(gq, gk):
    return (1.02 * HEAD_DIM * HEAD_DIM ** -0.5 * LOG2E) * jnp.max(jnp.abs(gq)) * jnp.max(jnp.abs(gk))


def _gain_tile(g, scale=1.0):
    return (jnp.tile(g.astype(F32), 2) * scale)[None, :]


def _even_w_in(w):
    hd = N_HEADS_A * HEAD_DIM
    cuts = np.cumsum([hd, hd, hd, N_HEADS_A, hd, hd, hd, IDX_HEADS * IDX_DIM, IDX_DIM, IDX_HEADS])[:-1]
    qa, ka, va, fa, qb, kb, vb, qi, ki, wi = jnp.split(w, [int(c) for c in cuts], axis=1)
    pad = jnp.zeros((w.shape[0], LANES - IDX_DIM - N_HEADS_A - IDX_HEADS), w.dtype)
    return jnp.concatenate([qa, ka, va, qb, kb, vb, qi, ki, fa, wi, pad], axis=1).astype(BF16)


def kernel(x, p, norm_mix, w_in_even, b_forget, q_norm_a, k_norm_a, q_norm_b, k_norm_b, w_in_odd, q_norm_c,
           k_norm_c, w_out, norm_ffn, w_gate, w_up, conv_w, conv_b, w_down, norm_ple, w_ple_gate, w_ple_proj):
    batch, s, d = x.shape
    assert batch == 1 and d == D_MODEL
    depth = norm_mix.shape[0]
    topk = min(TOPK_MAX, s // 4)
    t_attn = min(512, s)
    tm = min(512, s)
    tm_prep = min(256, s)
    qscale = HEAD_DIM ** -0.5 * LOG2E

    cos, sin = _rope_tables(s)
    lane = np.arange(LANES)
    bmat = jnp.asarray((lane[:, None] // HEAD_DIM == lane[None, :] // HEAD_DIM) / HEAD_DIM, BF16)
    tri = jnp.asarray(np.tril(np.ones((tm_prep, tm_prep))), BF16)

    xs = x[0]
    for i in range(depth):
        j = i // 2
        g_mix = norm_mix[i][None, :]
        w_o = w_out[i].astype(BF16)
        if i % 2 == 0:
            proj = _rms_matmul(xs, g_mix, _even_w_in(w_in_even[j]), tm=tm, tn=EV_WIDTH // 3)
            bf_pad = jnp.zeros((1, LANES), F32).at[0, MISC_F:MISC_F + N_HEADS_A].set(b_forget[j])
            bound_a = _logit_bound(q_norm_a[j], k_norm_a[j])
            bound_b = _logit_bound(q_norm_b[j], k_norm_b[j])
            shifts = jnp.stack([jnp.full((LANES,), LOGIT_CEIL_LOG2 - bound_a, F32),
                                jnp.full((LANES,), LOGIT_CEIL_LOG2 - bound_b, F32)])
            qa, ka, va_t, qb, kb, vb_t, qi, ki, wi_t, cs = _prep_even(
                proj, cos, sin, _gain_tile(q_norm_a[j], qscale), _gain_tile(k_norm_a[j]),
                _gain_tile(q_norm_b[j], qscale), _gain_tile(k_norm_b[j]), bf_pad, shifts, bmat, tri,
                tm=tm_prep)
            first_a = _first_live_tile(cs[:, MISC_F:MISC_F + N_HEADS_A], t_attn)
            out_a = _attention_any(qa, ka, va_t, None, bound_a, first_a, t=t_attn)
            mask = _indexer(qi, ki, wi_t, tq=LANES, tk=t_attn, topk=topk, group=16)
            out_b = _attention_any(qb, kb, vb_t, mask, bound_b, jnp.zeros_like(first_a), t=t_attn)
            half = N_HEADS_A * HEAD_DIM
            xs = _matmul_res([(out_a, w_o[:half]), (out_b, w_o[half:])], xs, tm=tm)
        else:
            proj = _rms_matmul(xs, g_mix, w_in_odd[j].astype(BF16), tm=tm, tn=D_MODEL)
            q, k, v = _prep_odd(proj, cos, sin, _gain_tile(q_norm_c[j], qscale), _gain_tile(k_norm_c[j]),
                                bmat, tm=tm_prep)
            xs = _matmul_res([(_dilated(q, k, v), w_o)], xs, tm=tm)
        act = _ffn_act(xs, norm_ffn[i][None, :], w_gate[i].astype(BF16), w_up[i].astype(BF16),
                       conv_w[i], conv_b[i][None, :], tm=tm_prep, tn=D_FF // 2)
        xs = _matmul_res([(act, w_down[i].astype(BF16))], xs, tm=tm)
        xs = _ple(xs, p[i, 0], norm_ple[i][None, :], w_ple_gate[i].astype(BF16),
                  w_ple_proj[i].astype(BF16), tm=tm)
    return xs[None]
```

```python
import functools

import numpy as np
import jax
import jax.numpy as jnp
from jax import lax
from jax.experimental import pallas as pl
from jax.experimental.pallas import tpu as pltpu

F32 = jnp.float32
BF16 = jnp.bfloat16

D_MODEL = 1024
HEAD_DIM = 64
N_HEADS = D_MODEL // HEAD_DIM
N_HEADS_A = N_HEADS // 2
N_HEADS_B = N_HEADS - N_HEADS_A
IDX_HEADS = 16
IDX_DIM = 64
TOPK_MAX = 256
DIL_PATTERNS = ((128, 1), (512, 4), (2048, 16))
Q_BLOCK = 128
D_FF = 2816
P_DIM = 256
ROPE_THETA = 10000.0
EPS = 1e-6
DEPTH = 4

LANES = 128
MXU_WIDTH = 256
LOG2E = 1.4426950408889634
NEG_INF = float("-inf")
INT_MIN = -(2 ** 31)
VMEM_LIMIT_BYTES = 56 * 1024 * 1024
LOGIT_CEIL_LOG2 = 40.0
LOGIT_BOUND_MAX = 80.0
LOGIT_DEAD = LOGIT_CEIL_LOG2 + 160.0

EV_QA, EV_KA, EV_VA, EV_QB, EV_KB, EV_VB, EV_QI, EV_MISC = 0, 512, 1024, 1536, 2048, 2560, 3072, 4096
EV_WIDTH = 4224
MISC_F = 64
MISC_W = 72

_NT = (((1,), (1,)), ((), ()))


def _params(n_axes):
    return pltpu.CompilerParams(dimension_semantics=("arbitrary",) * n_axes,
                                vmem_limit_bytes=VMEM_LIMIT_BYTES)


def _rmsnorm_rows(x, g):
    ms = jnp.mean(x * x, axis=-1, keepdims=True)
    return x * lax.rsqrt(ms + EPS) * g


def _split3(x):
    hi = x.astype(BF16)
    r = x - hi.astype(F32)
    mid = r.astype(BF16)
    lo = (r - mid.astype(F32)).astype(BF16)
    return hi, mid, lo


def _rms_matmul_kernel(x_ref, g_ref, w_ref, o_ref):
    h = _rmsnorm_rows(x_ref[...], g_ref[...]).astype(BF16)
    o_ref[...] = jnp.dot(h, w_ref[...], preferred_element_type=F32)


def _rms_matmul(x, g, w, *, tm, tn):
    s, d = x.shape
    n = w.shape[1]
    return pl.pallas_call(
        _rms_matmul_kernel,
        grid=(n // tn, s // tm),
        in_specs=[pl.BlockSpec((tm, d), lambda j, i: (i, 0)),
                  pl.BlockSpec((1, d), lambda j, i: (0, 0)),
                  pl.BlockSpec((d, tn), lambda j, i: (0, j))],
        out_specs=pl.BlockSpec((tm, tn), lambda j, i: (i, j)),
        out_shape=jax.ShapeDtypeStruct((s, n), F32),
        compiler_params=_params(2),
        name="rms_matmul",
    )(x, g, w)


def _head_rmsnorm(x, gain, bmat):
    sq = x * x
    hi = sq.astype(BF16)
    lo = (sq - hi.astype(F32)).astype(BF16)
    ms = (jnp.dot(hi, bmat, preferred_element_type=F32)
          + jnp.dot(lo, bmat, preferred_element_type=F32))
    return x * lax.rsqrt(ms + EPS) * gain


def _rope(x, cos, sin_signed, low_half):
    rot = jnp.where(low_half, pltpu.roll(x, LANES - 32, 1), pltpu.roll(x, 32, 1))
    return x * cos + rot * sin_signed


def _log_sigmoid(z):
    return jnp.minimum(z, 0.0) - jnp.log1p(jnp.exp(-jnp.abs(z)))


def _prep_even_kernel(proj_ref, cos_ref, sin_ref, gqa_ref, gka_ref, gqb_ref, gkb_ref, bf_ref,
                      shift_ref, bmat_ref, tri_ref,
                      qa_ref, ka_ref, va_ref, qb_ref, kb_ref, vb_ref, qi_ref, ki_ref, wi_ref, cs_ref, carry):
    tm = proj_ref.shape[0]
    shift_a = shift_ref[0:1, :]
    shift_b = shift_ref[1:2, :]

    @pl.when(pl.program_id(0) == 0)
    def _():
        carry[...] = jnp.zeros_like(carry)

    cos = cos_ref[...]
    sin = sin_ref[...]
    bmat = bmat_ref[...]
    lane = lax.broadcasted_iota(jnp.int32, (tm, LANES), 1)
    low_half = (lane & 32) == 0
    head0 = lane < HEAD_DIM

    misc = proj_ref[:, EV_MISC:EV_MISC + LANES]

    logf = _log_sigmoid(misc + bf_ref[...])
    logf = jnp.where((lane >= MISC_F) & (lane < MISC_F + N_HEADS_A), logf, 0.0)
    tri = tri_ref[...]
    f_hi, f_mid, f_lo = _split3(logf)
    c = (jnp.dot(tri, f_hi, preferred_element_type=F32)
         + jnp.dot(tri, f_mid, preferred_element_type=F32)
         + jnp.dot(tri, f_lo, preferred_element_type=F32)) + carry[0:1, :]
    carry[0:1, :] = c[tm - 1:tm, :]
    cs_ref[...] = c * LOG2E
    c_hi, c_mid, c_lo = (t.astype(F32) for t in _split3(c * LOG2E))

    def place(block, h):
        return block if h % 2 == 0 else pltpu.roll(block, HEAD_DIM, 1)

    for hp in range(N_HEADS_A // 2):
        cols = slice(hp * LANES, (hp + 1) * LANES)
        qa = _head_rmsnorm(proj_ref[:, EV_QA + hp * LANES:EV_QA + (hp + 1) * LANES], gqa_ref[...], bmat)
        ka = _head_rmsnorm(proj_ref[:, EV_KA + hp * LANES:EV_KA + (hp + 1) * LANES], gka_ref[...], bmat)
        va = proj_ref[:, EV_VA + hp * LANES:EV_VA + (hp + 1) * LANES]
        qb = _rope(_head_rmsnorm(proj_ref[:, EV_QB + hp * LANES:EV_QB + (hp + 1) * LANES], gqb_ref[...], bmat),
                   cos, sin, low_half)
        kb = _rope(_head_rmsnorm(proj_ref[:, EV_KB + hp * LANES:EV_KB + (hp + 1) * LANES], gkb_ref[...], bmat),
                   cos, sin, low_half)
        vb = proj_ref[:, EV_VB + hp * LANES:EV_VB + (hp + 1) * LANES]
        del cols
        for sub in range(2):
            h = 2 * hp + sub
            src = MISC_F + h
            q_aug = jnp.where(
                head0, place(qa, h),
                jnp.where(lane == 64, pltpu.roll(c_hi, (64 - src) % LANES, 1),
                          jnp.where(lane == 65, pltpu.roll(c_mid, (65 - src) % LANES, 1),
                                    jnp.where(lane == 66, pltpu.roll(c_lo, (66 - src) % LANES, 1),
                                              jnp.where(lane < 70, 1.0,
                                                        jnp.where(lane == 70, shift_a, 0.0))))))
            k_aug = jnp.where(
                head0, place(ka, h),
                jnp.where(lane < 67, 1.0,
                          jnp.where(lane == 67, -pltpu.roll(c_hi, (67 - src) % LANES, 1),
                                    jnp.where(lane == 68, -pltpu.roll(c_mid, (68 - src) % LANES, 1),
                                              jnp.where(lane == 69, -pltpu.roll(c_lo, (69 - src) % LANES, 1),
                                                        jnp.where(lane == 70, 1.0, 0.0))))))
            ones_col = jnp.where(lane == HEAD_DIM, 1.0, 0.0)
            qa_ref[h] = q_aug.astype(BF16)
            ka_ref[h] = k_aug.astype(BF16)
            va_ref[h] = jnp.where(head0, place(va, h), ones_col).T.astype(BF16)
            qb_ref[h] = jnp.where(head0, place(qb, h), jnp.where(lane == HEAD_DIM, shift_b, 0.0)).astype(BF16)
            kb_ref[h] = jnp.where(head0, place(kb, h), ones_col).astype(BF16)
            vb_ref[h] = jnp.where(head0, place(vb, h), ones_col).T.astype(BF16)

    for blk in range(IDX_HEADS * IDX_DIM // LANES):
        cols = slice(EV_QI + blk * LANES, EV_QI + (blk + 1) * LANES)
        qi = _rope(proj_ref[:, cols], cos, sin, low_half) * (IDX_DIM ** -0.5)
        qi_ref[2 * blk] = qi[:, 0:IDX_DIM].astype(BF16)
        qi_ref[2 * blk + 1] = qi[:, IDX_DIM:2 * IDX_DIM].astype(BF16)
    ki_ref[...] = _rope(misc, cos, sin, low_half)[:, 0:IDX_DIM].astype(BF16)
    wi_ref[...] = (misc * (IDX_HEADS ** -0.5)).T


def _prep_even(proj, cos, sin, gqa, gka, gqb, gkb, bf_pad, shifts, bmat, tri, *, tm):
    s = proj.shape[0]
    row = lambda w: pl.BlockSpec((tm, w), lambda i: (i, 0))
    const = lambda a: pl.BlockSpec(a.shape, lambda i: (0, 0))
    heads = pl.BlockSpec((N_HEADS_A, tm, LANES), lambda i: (0, i, 0))
    heads_shape = jax.ShapeDtypeStruct((N_HEADS_A, s, LANES), BF16)
    heads_t = pl.BlockSpec((N_HEADS_A, LANES, tm), lambda i: (0, 0, i))
    heads_t_shape = jax.ShapeDtypeStruct((N_HEADS_A, LANES, s), BF16)
    return pl.pallas_call(
        _prep_even_kernel,
        grid=(s // tm,),
        in_specs=[row(EV_WIDTH), row(LANES), row(LANES), const(gqa), const(gka), const(gqb), const(gkb),
                  const(bf_pad), const(shifts), const(bmat), const(tri)],
        out_specs=[heads, heads, heads_t, heads, heads, heads_t,
                   pl.BlockSpec((IDX_HEADS, tm, IDX_DIM), lambda i: (0, i, 0)),
                   row(IDX_DIM),
                   pl.BlockSpec((LANES, tm), lambda i: (0, i)),
                   row(LANES)],
        out_shape=[heads_shape, heads_shape, heads_t_shape, heads_shape, heads_shape, heads_t_shape,
                   jax.ShapeDtypeStruct((IDX_HEADS, s, IDX_DIM), BF16),
                   jax.ShapeDtypeStruct((s, IDX_DIM), BF16),
                   jax.ShapeDtypeStruct((LANES, s), F32),
                   jax.ShapeDtypeStruct((s, LANES), F32)],
        scratch_shapes=[pltpu.VMEM((8, LANES), F32)],
        compiler_params=_params(1),
        name="prep_even",
    )(proj, cos, sin, gqa, gka, gqb, gkb, bf_pad, shifts, bmat, tri)


def _float_key(x):
    bits = pltpu.bitcast(x, jnp.int32)
    return bits ^ ((bits >> 31) & 0x7FFFFFFF)


def _key_float(k):
    return pltpu.bitcast(k ^ ((k >> 31) & 0x7FFFFFFF), F32)


def _kth_search(count_fn, lo_k, hi_k, c_lo, done, kf, *, fixed_steps=None, steps_per_check=1,
                stop_on_count=True):
    def finished(lo, hi, c, d):
        d = d | (lo + 1 >= hi)
        return jnp.where(d | (c == kf) if stop_on_count else d, 1.0, 0.0)

    def halve(lo, hi, c, d):
        live = d == 0.0
        cand = (lo >> 1) + (hi >> 1) + (lo & hi & 1)
        n = count_fn(_key_float(cand))
        take = (n >= kf) & live
        drop = (n < kf) & live
        lo = jnp.where(take, cand, lo)
        c = jnp.where(take, n, c)
        hi = jnp.where(drop, cand, hi)
        return lo, hi, c, finished(lo, hi, c, d > 0.0)

    d0 = finished(lo_k, hi_k, c_lo, done)
    if fixed_steps is not None:
        lo_k, _, c_lo, _ = lax.fori_loop(0, fixed_steps, lambda _, st: halve(*st), (lo_k, hi_k, c_lo, d0))
        return lo_k, c_lo

    def body(state):
        st = state[:4]
        for _ in range(steps_per_check):
            st = halve(*st)
        return (*st, jnp.max(1.0 - st[3]))

    lo_k, _, c_lo, _, _ = lax.while_loop(lambda state: state[4] > 0.0, body,
                                         (lo_k, hi_k, c_lo, d0, jnp.max(1.0 - d0)))
    return lo_k, c_lo


def _indexer_kernel(qi_ref, ki_ref, wi_ref, bias_ref, sc_scr, g_scr, *, tq, tk, topk, group):
    i = pl.program_id(0)
    s_keys = sc_scr.shape[0]
    n_chunks = s_keys // tk
    nk = ((i + 1) * tq + tk - 1) // tk
    sub = 8
    g_rows = tk // group
    heads_per_dot = max(1, MXU_WIDTH // tq)
    kf = float(topk)
    lowest = float(np.finfo(np.float32).min)
    qpos = lax.broadcasted_iota(jnp.int32, (1, tq), 1) + i * tq
    w = wi_ref[...]

    g_scr[...] = jnp.full(g_scr.shape, NEG_INF, F32)

    def score_chunk(c, gmax):
        ks = pl.multiple_of(c * tk, tk)
        ki = ki_ref[pl.ds(ks, tk), :]
        acc = jnp.zeros((tk, tq), F32)
        for first_head in range(0, IDX_HEADS, heads_per_dot):
            q_rows = [qi_ref[first_head + u] for u in range(heads_per_dot)]
            q_cat = q_rows[0] if heads_per_dot == 1 else jnp.concatenate(q_rows, axis=0)
            logits = lax.dot_general(ki, q_cat, _NT, preferred_element_type=F32)
            for u in range(heads_per_dot):
                w_h = w[MISC_W + first_head + u:MISC_W + first_head + u + 1, :]
                acc = acc + w_h * jnp.maximum(logits[:, u * tq:(u + 1) * tq], 0.0)
        kpos = lax.broadcasted_iota(jnp.int32, (tk, tq), 0) + c * tk
        sc = jnp.where(kpos <= qpos, acc, NEG_INF)
        sc_scr[pl.ds(ks, tk), :] = sc
        g = jnp.max(sc.reshape(tk // (group * sub), group, sub, tq), axis=1).reshape(g_rows, tq)
        g_scr[pl.ds(pl.multiple_of(c * g_rows, g_rows), g_rows), :] = g
        return jnp.maximum(gmax, jnp.max(g, axis=0, keepdims=True))

    gmax = lax.fori_loop(0, nk, score_chunk, jnp.full((1, tq), NEG_INF, F32))

    def count_rows(ref, rows_per_step, n_steps, cand):
        ways = min(8, rows_per_step // sub)

        def step(c, cnt):
            r0 = pl.multiple_of(c * rows_per_step, rows_per_step)
            ones = jnp.where(ref[pl.ds(r0, rows_per_step), :] >= cand, 1.0, 0.0)
            return cnt + jnp.sum(ones.reshape(rows_per_step // (ways * sub), ways, sub, tq), axis=0)

        cnt = lax.fori_loop(0, n_steps, step, jnp.zeros((ways, sub, tq), F32))
        return jnp.sum(jnp.sum(cnt, axis=0), axis=0, keepdims=True)

    g_piece = min(256, g_scr.shape[0])
    count_groups = lambda cand: count_rows(g_scr, g_piece, (nk * g_rows + g_piece - 1) // g_piece, cand)
    count_all = lambda cand: count_rows(sc_scr, tk, nk, cand)

    lo0 = _float_key(jnp.full((1, tq), lowest, F32))
    hi0 = _float_key(gmax) + 1
    few = qpos < topk
    c_groups = count_groups(jnp.full((1, tq), lowest, F32))
    u_k, _ = _kth_search(count_groups, lo0, hi0, c_groups, few | (c_groups < kf), kf, fixed_steps=32)
    u = _key_float(u_k)
    c_u = count_all(u)
    n_max = count_groups(u)
    rank = kf - (c_u - n_max)
    has_y = jnp.logical_not(few) & (rank >= 1.0) & (rank <= n_max)
    y_k, _ = _kth_search(count_groups, u_k, hi0, n_max, jnp.logical_not(has_y), rank, fixed_steps=32,
                         stop_on_count=False)
    hi1 = jnp.where(has_y, y_k + 1, hi0)
    t_k, _ = _kth_search(count_all, u_k, hi1, c_u, few, kf, steps_per_check=2)
    thr = jnp.where(few, lowest, _key_float(t_k))

    def bias_chunk(c, carry):
        ks = pl.multiple_of(c * tk, tk)
        bias_ref[pl.ds(ks, tk), :] = jnp.where(sc_scr[pl.ds(ks, tk), :] >= thr, 1.0, 0.0).astype(BF16)
        return carry

    lax.fori_loop(0, nk, bias_chunk, 0)

    def fill_chunk(c, carry):
        bias_ref[pl.ds(pl.multiple_of(c * tk, tk), tk), :] = jnp.zeros((tk, tq), BF16)
        return carry

    lax.fori_loop(nk, n_chunks, fill_chunk, 0)


def _indexer(qi, ki, wi_t, *, tq, tk, topk, group):
    s = ki.shape[0]
    return pl.pallas_call(
        functools.partial(_indexer_kernel, tq=tq, tk=tk, topk=topk, group=group),
        grid=(s // tq,),
        in_specs=[pl.BlockSpec((IDX_HEADS, tq, IDX_DIM), lambda i: (0, i, 0)),
                  pl.BlockSpec((s, IDX_DIM), lambda i: (0, 0)),
                  pl.BlockSpec((LANES, tq), lambda i: (0, i))],
        out_specs=pl.BlockSpec((None, s, tq), lambda i: (i, 0, 0)),
        out_shape=jax.ShapeDtypeStruct((s // tq, s, tq), BF16),
        scratch_shapes=[pltpu.VMEM((s, tq), F32), pltpu.VMEM((s // group, tq), F32)],
        compiler_params=_params(1),
        name="dsa_indexer",
    )(qi, ki, wi_t)


def _attn_kernel(*refs, has_mask, bounded):
    first_ref, q_ref, k_ref, vt_ref = refs[:4]
    refs = refs[1:]
    mask_ref = refs[3] if has_mask else None
    o_ref = refs[3 + has_mask]
    scratch = refs[4 + has_mask:]
    acc_scr = scratch[-1]
    m_scr = None if bounded else scratch[0]
    n_heads, t, _ = q_ref.shape
    i = pl.program_id(0)
    j = pl.program_id(1)

    @pl.when(j == 0)
    def _():
        acc_scr[...] = jnp.zeros(acc_scr.shape, F32)
        if not bounded:
            m_scr[...] = jnp.full(m_scr.shape, NEG_INF, F32)

    def step(diagonal):
        if has_mask:
            mask = jnp.concatenate([mask_ref[b] for b in range(mask_ref.shape[0])], axis=1)
            if not bounded:
                selected = mask.astype(F32) > 0.0
        if diagonal:
            causal = (lax.broadcasted_iota(jnp.int32, (t, t), 0)
                      <= lax.broadcasted_iota(jnp.int32, (t, t), 1))

        def logits(h):
            return lax.dot_general(k_ref[h], q_ref[h], _NT, preferred_element_type=F32)

        s_next = logits(0)
        for h in range(n_heads):
            s = s_next
            if h + 1 < n_heads:
                s_next = logits(h + 1)
            if diagonal:
                s = jnp.where(causal, s, NEG_INF)
            if bounded:
                p = jnp.exp2(s).astype(BF16)
                if has_mask:
                    p = p * mask
                acc_scr[h] = acc_scr[h] + jnp.dot(vt_ref[h], p, preferred_element_type=F32)
            else:
                if has_mask:
                    s = jnp.where(selected, s, NEG_INF)
                m_old = m_scr[h][0:1, :]
                m_new = jnp.maximum(m_old, jnp.max(s, axis=0, keepdims=True))
                m_safe = jnp.where(m_new == NEG_INF, 0.0, m_new)
                p = jnp.exp2(s - m_safe)
                alpha = jnp.exp2(m_old - m_safe)
                acc_scr[h] = alpha * acc_scr[h] + jnp.dot(vt_ref[h], p.astype(BF16), preferred_element_type=F32)
                m_scr[h] = jnp.broadcast_to(m_new, m_scr.shape[1:])

    @pl.when((j < i) & (j >= first_ref[i]))
    def _():
        step(False)

    @pl.when(j == i)
    def _():
        step(True)
        for hp in range(n_heads // 2):
            halves = []
            for h in (2 * hp, 2 * hp + 1):
                a = acc_scr[h]
                halves.append(a[0:HEAD_DIM, :] / a[HEAD_DIM:HEAD_DIM + 1, :])
            o_ref[:, hp * LANES:(hp + 1) * LANES] = jnp.concatenate(halves, axis=0).T.astype(o_ref.dtype)


def _attention(q, k, vt, mask, first, *, t, bounded):
    n_heads, s, _ = q.shape
    sub = 8
    key_tile = lambda i, j, first_ref: jnp.clip(j, first_ref[i], i)
    in_specs = [pl.BlockSpec((n_heads, t, LANES), lambda i, j, f: (0, i, 0)),
                pl.BlockSpec((n_heads, t, LANES), lambda i, j, f: (0, key_tile(i, j, f), 0)),
                pl.BlockSpec((n_heads, LANES, t), lambda i, j, f: (0, 0, key_tile(i, j, f)))]
    args = [q, k, vt]
    if mask is not None:
        tq_mask = mask.shape[2]
        in_specs.append(pl.BlockSpec((t // tq_mask, t, tq_mask), lambda i, j, f: (i, key_tile(i, j, f), 0)))
        args.append(mask)
    scratch = [pltpu.VMEM((n_heads, LANES, t), F32)]
    if not bounded:
        scratch.insert(0, pltpu.VMEM((n_heads, sub, t), F32))
    name = ("dsa_attention" if mask is not None else "fox_attention") + ("" if bounded else "_online")
    return pl.pallas_call(
        functools.partial(_attn_kernel, has_mask=mask is not None, bounded=bounded),
        grid_spec=pltpu.PrefetchScalarGridSpec(
            num_scalar_prefetch=1,
            grid=(s // t, s // t),
            in_specs=in_specs,
            out_specs=pl.BlockSpec((t, n_heads * HEAD_DIM), lambda i, j, f: (i, 0)),
            scratch_shapes=scratch),
        out_shape=jax.ShapeDtypeStruct((s, n_heads * HEAD_DIM), BF16),
        compiler_params=_params(2),
        name=name,
    )(first, *args)


def _attention_any(q, k, vt, mask, logit_bound, first, *, t):
    operands = (q, k, vt, first) if mask is None else (q, k, vt, first, mask)
    m = lambda ops: ops[4] if mask is not None else None
    return lax.cond(logit_bound <= LOGIT_BOUND_MAX,
                    lambda *ops: _attention(ops[0], ops[1], ops[2], m(ops), ops[3], t=t, bounded=True),
                    lambda *ops: _attention(ops[0], ops[1], ops[2], m(ops), jnp.zeros_like(ops[3]), t=t,
                                            bounded=False),
                    *operands)


def _first_live_tile(cs, t):
    top = cs[0::t]
    bottom = cs[t - 1::t]
    live = jnp.any(top[:, None, :] - bottom[None, :, :] > -LOGIT_DEAD, axis=-1)
    return jnp.argmax(live, axis=1).astype(jnp.int32)


def _matmul_res_kernel(*refs, n_pairs):
    res_ref = refs[2 * n_pairs]
    o_ref = refs[2 * n_pairs + 1]
    acc = res_ref[...]
    for p in range(n_pairs):
        acc = acc + jnp.dot(refs[2 * p][...], refs[2 * p + 1][...], preferred_element_type=F32)
    o_ref[...] = acc


def _matmul_res(pairs, res, *, tm):
    s, n = res.shape
    in_specs, args = [], []
    for a, w in pairs:
        in_specs += [pl.BlockSpec((tm, a.shape[1]), lambda i: (i, 0)),
                     pl.BlockSpec(w.shape, lambda i: (0, 0))]
        args += [a, w]
    in_specs.append(pl.BlockSpec((tm, n), lambda i: (i, 0)))
    args.append(res)
    return pl.pallas_call(
        functools.partial(_matmul_res_kernel, n_pairs=len(pairs)),
        grid=(s // tm,),
        in_specs=in_specs,
        out_specs=pl.BlockSpec((tm, n), lambda i: (i, 0)),
        out_shape=jax.ShapeDtypeStruct((s, n), F32),
        compiler_params=_params(1),
        name="matmul_residual",
    )(*args)


def _prep_odd_kernel(proj_ref, cos_ref, sin_ref, gq_ref, gk_ref, bmat_ref, q_ref, k_ref, v_ref):
    tm = proj_ref.shape[0]
    cos = cos_ref[...]
    sin = sin_ref[...]
    bmat = bmat_ref[...]
    lane = lax.broadcasted_iota(jnp.int32, (tm, LANES), 1)
    low_half = (lane & 32) == 0
    for blk in range(D_MODEL // LANES):
        cols = slice(blk * LANES, (blk + 1) * LANES)
        q = proj_ref[:, blk * LANES:(blk + 1) * LANES]
        k = proj_ref[:, D_MODEL + blk * LANES:D_MODEL + (blk + 1) * LANES]
        q_ref[:, cols] = _rope(_head_rmsnorm(q, gq_ref[...], bmat), cos, sin, low_half)
        k_ref[:, cols] = _rope(_head_rmsnorm(k, gk_ref[...], bmat), cos, sin, low_half)
        v_ref[:, cols] = proj_ref[:, 2 * D_MODEL + blk * LANES:2 * D_MODEL + (blk + 1) * LANES]


def _prep_odd(proj, cos, sin, gq, gk, bmat, *, tm):
    s = proj.shape[0]
    row = lambda w: pl.BlockSpec((tm, w), lambda i: (i, 0))
    const = lambda a: pl.BlockSpec(a.shape, lambda i: (0, 0))
    shape = jax.ShapeDtypeStruct((s, D_MODEL), F32)
    return pl.pallas_call(
        _prep_odd_kernel,
        grid=(s // tm,),
        in_specs=[row(3 * D_MODEL), row(LANES), row(LANES), const(gq), const(gk), const(bmat)],
        out_specs=[row(D_MODEL)] * 3,
        out_shape=[shape] * 3,
        compiler_params=_params(1),
        name="prep_odd",
    )(proj, cos, sin, gq, gk, bmat)


def _band_head_pair(q2, k2, v2, ok, first, first2):
    zero = jnp.zeros((), BF16)
    one = jnp.ones((), BF16)
    res, mx = [], []
    for sub in range(2):
        mine = first if sub == 0 else jnp.logical_not(first)
        mine2 = first2 if sub == 0 else jnp.logical_not(first2)
        s = lax.dot_general(jnp.where(mine, q2, zero), k2, _NT, preferred_element_type=F32)
        s = jnp.where(ok, s, NEG_INF)
        m = jnp.max(s, axis=-1, keepdims=True)
        p = jnp.exp2(s - m).astype(BF16)
        res.append(jnp.dot(p, jnp.where(mine2, v2, one), preferred_element_type=F32))
        mx.append(m)
    num = jnp.where(first, res[0], res[1])
    den_sw = jnp.where(first, res[1], res[0])
    den = pltpu.roll(den_sw, HEAD_DIM, 1)
    return num / den, jnp.where(first, mx[0], mx[1]) + jnp.log2(den)


def _dilated_kernel(q_ref, kc_ref, kp_ref, vc_ref, vp_ref, o_ref, o_scr, l_scr):
    win = q_ref.shape[0]
    t = Q_BLOCK
    qi = lax.broadcasted_iota(jnp.int32, (t, t), 0)
    kj = lax.broadcasted_iota(jnp.int32, (t, t), 1)
    ok_inner = jnp.concatenate([kj >= qi, kj <= qi], axis=1)
    ok_first = jnp.concatenate([(kj >= qi) & (pl.program_id(0) > 0), kj <= qi], axis=1)
    first = lax.broadcasted_iota(jnp.int32, (t, LANES), 1) < HEAD_DIM
    first2 = lax.broadcasted_iota(jnp.int32, (2 * t, LANES), 1) < HEAD_DIM

    for p, (_, dil) in enumerate(DIL_PATTERNS):
        span = dil * t
        n_blocks = win // span
        for r in range(dil):
            for cb in range(n_blocks):
                rows = pl.ds(r + cb * span, t, stride=dil) if dil > 1 else pl.ds(cb * t, t)
                if cb > 0:
                    prows = pl.ds(r + (cb - 1) * span, t, stride=dil) if dil > 1 else pl.ds((cb - 1) * t, t)
                    k_prev, v_prev, ok = kc_ref[prows, :], vc_ref[prows, :], ok_inner
                else:
                    prows = (pl.ds(r + (n_blocks - 1) * span, t, stride=dil) if dil > 1
                             else pl.ds((n_blocks - 1) * t, t))
                    k_prev, v_prev, ok = kp_ref[prows, :], vp_ref[prows, :], ok_first
                k2 = jnp.concatenate([k_prev, kc_ref[rows, :]], axis=0).astype(BF16)
                v2 = jnp.concatenate([v_prev, vc_ref[rows, :]], axis=0).astype(BF16)
                o2, lse2 = _band_head_pair(q_ref[rows, :].astype(BF16), k2, v2, ok, first, first2)
                o_scr[p, rows, :] = o2
                l_scr[p, rows, :] = lse2

    l1, l2, l3 = l_scr[0], l_scr[1], l_scr[2]
    m = jnp.maximum(jnp.maximum(l1, l2), l3)
    w1, w2, w3 = jnp.exp2(l1 - m), jnp.exp2(l2 - m), jnp.exp2(l3 - m)
    merged = (w1 * o_scr[0] + w2 * o_scr[1] + w3 * o_scr[2]) / (w1 + w2 + w3)
    o_ref[...] = merged.astype(o_ref.dtype)


def _dilated(q, k, v):
    s = q.shape[0]
    win = max(dil for _, dil in DIL_PATTERNS) * Q_BLOCK
    assert s % win == 0
    cur = pl.BlockSpec((win, LANES), lambda c, j: (c, j))
    prev = pl.BlockSpec((win, LANES), lambda c, j: (jnp.maximum(c - 1, 0), j))
    return pl.pallas_call(
        _dilated_kernel,
        grid=(s // win, D_MODEL // LANES),
        in_specs=[cur, cur, prev, cur, prev],
        out_specs=cur,
        out_shape=jax.ShapeDtypeStruct((s, D_MODEL), BF16),
        scratch_shapes=[pltpu.VMEM((len(DIL_PATTERNS), win, LANES), F32),
                        pltpu.VMEM((len(DIL_PATTERNS), win, LANES), F32)],
        compiler_params=_params(2),
        name="dilated_attention",
    )(q, k, k, v, v)


def _gelu_tanh(x):
    return 0.5 * x * (1.0 + jnp.tanh(np.sqrt(2.0 / np.pi).astype(np.float32) * (x + 0.044715 * (x * x * x))))


def _ffn_act_kernel(x_ref, halo_ref, g_ref, wg_ref, wu_ref, cw_ref, cb_ref, o_ref):
    i = pl.program_id(1)
    tm = x_ref.shape[0]
    h = _rmsnorm_rows(x_ref[...], g_ref[...]).astype(BF16)
    hh = _rmsnorm_rows(halo_ref[...], g_ref[...]).astype(BF16)
    wg = wg_ref[...]
    gate = jnp.dot(h, wg, preferred_element_type=F32)
    gate_halo = jnp.where(i > 0, jnp.dot(hh, wg, preferred_element_type=F32), 0.0)
    up = jnp.dot(h, wu_ref[...], preferred_element_type=F32)
    row = lax.broadcasted_iota(jnp.int32, gate.shape, 0)
    g_m1 = jnp.where(row == 0, gate_halo[7:8, :], pltpu.roll(gate, 1, 0))
    g_m2 = jnp.where(row == 0, gate_halo[6:7, :],
                     jnp.where(row == 1, gate_halo[7:8, :], pltpu.roll(gate, 2, 0)))
    cw = cw_ref[...]
    conv = cw[2:3, :] * gate + cw[1:2, :] * g_m1 + cw[0:1, :] * g_m2 + cb_ref[...]
    o_ref[...] = (_gelu_tanh(conv) * up).astype(o_ref.dtype)
    del tm


def _ffn_act(x, g, wg, wu, cw, cb, *, tm, tn):
    s, d = x.shape
    n = wg.shape[1]
    halo_rows = 8
    return pl.pallas_call(
        _ffn_act_kernel,
        grid=(n // tn, s // tm),
        in_specs=[pl.BlockSpec((tm, d), lambda j, i: (i, 0)),
                  pl.BlockSpec((halo_rows, d), lambda j, i: (jnp.maximum(i * (tm // halo_rows) - 1, 0), 0)),
                  pl.BlockSpec((1, d), lambda j, i: (0, 0)),
                  pl.BlockSpec((d, tn), lambda j, i: (0, j)),
                  pl.BlockSpec((d, tn), lambda j, i: (0, j)),
                  pl.BlockSpec((cw.shape[0], tn), lambda j, i: (0, j)),
                  pl.BlockSpec((1, tn), lambda j, i: (0, j))],
        out_specs=pl.BlockSpec((tm, tn), lambda j, i: (i, j)),
        out_shape=jax.ShapeDtypeStruct((s, n), BF16),
        compiler_params=_params(2),
        name="ffn_act",
    )(x, x, g, wg, wu, cw, cb)


def _ple_kernel(x_ref, p_ref, g_ref, wg_ref, wp_ref, o_ref):
    x = x_ref[...]
    z = jnp.dot(_rmsnorm_rows(x, g_ref[...]).astype(BF16), wg_ref[...], preferred_element_type=F32)
    gate = 1.0 / (1.0 + jnp.exp(-z))
    proj = jnp.dot(p_ref[...].astype(BF16), wp_ref[...], preferred_element_type=F32)
    o_ref[...] = x + gate * proj


def _ple(x, p, g, wg, wp, *, tm):
    s, d = x.shape
    return pl.pallas_call(
        _ple_kernel,
        grid=(s // tm,),
        in_specs=[pl.BlockSpec((tm, d), lambda i: (i, 0)),
                  pl.BlockSpec((tm, p.shape[1]), lambda i: (i, 0)),
                  pl.BlockSpec((1, d), lambda i: (0, 0)),
                  pl.BlockSpec(wg.shape, lambda i: (0, 0)),
                  pl.BlockSpec(wp.shape, lambda i: (0, 0))],
        out_specs=pl.BlockSpec((tm, d), lambda i: (i, 0)),
        out_shape=jax.ShapeDtypeStruct((s, d), F32),
        compiler_params=_params(1),
        name="ple",
    )(x, p, g, wg, wp)


def _rope_tables(s):
    half = HEAD_DIM // 2
    inv_freq = ROPE_THETA ** (-jnp.arange(half, dtype=F32) / half)
    ang = jnp.arange(s).astype(F32)[:, None] * inv_freq[None, :]
    cos, sin = jnp.cos(ang), jnp.sin(ang)
    return jnp.tile(cos, (1, 4)), jnp.tile(jnp.concatenate([-sin, sin], axis=1), (1, 2))


def _logit_bound(gq, gk):
    return (1.02 * HEAD_DIM * HEAD_DIM ** -0.5 * LOG2E) * jnp.max(jnp.abs(gq)) * jnp.max(jnp.abs(gk))


def _gain_tile(g, scale=1.0):
    return (jnp.tile(g.astype(F32), 2) * scale)[None, :]


def _even_w_in(w):
    hd = N_HEADS_A * HEAD_DIM
    cuts = np.cumsum([hd, hd, hd, N_HEADS_A, hd, hd, hd, IDX_HEADS * IDX_DIM, IDX_DIM, IDX_HEADS])[:-1]
    qa, ka, va, fa, qb, kb, vb, qi, ki, wi = jnp.split(w, [int(c) for c in cuts], axis=1)
    pad = jnp.zeros((w.shape[0], LANES - IDX_DIM - N_HEADS_A - IDX_HEADS), w.dtype)
    return jnp.concatenate([qa, ka, va, qb, kb, vb, qi, ki, fa, wi, pad], axis=1).astype(BF16)


def kernel(x, p, norm_mix, w_in_even, b_forget, q_norm_a, k_norm_a, q_norm_b, k_norm_b, w_in_odd, q_norm_c,
           k_norm_c, w_out, norm_ffn, w_gate, w_up, conv_w, conv_b, w_down, norm_ple, w_ple_gate, w_ple_proj):
    batch, s, d = x.shape
    assert batch == 1 and d == D_MODEL
    depth = norm_mix.shape[0]
    topk = min(TOPK_MAX, s // 4)
    t_attn = min(512, s)
    tm = min(512, s)
    tm_prep = min(256, s)
    qscale = HEAD_DIM ** -0.5 * LOG2E

    cos, sin = _rope_tables(s)
    lane = np.arange(LANES)
    bmat = jnp.asarray((lane[:, None] // HEAD_DIM == lane[None, :] // HEAD_DIM) / HEAD_DIM, BF16)
    tri = jnp.asarray(np.tril(np.ones((tm_prep, tm_prep))), BF16)

    xs = x[0]
    for i in range(depth):
        j = i // 2
        g_mix = norm_mix[i][None, :]
        w_o = w_out[i].astype(BF16)
        if i % 2 == 0:
            proj = _rms_matmul(xs, g_mix, _even_w_in(w_in_even[j]), tm=tm, tn=EV_WIDTH // 3)
            bf_pad = jnp.zeros((1, LANES), F32).at[0, MISC_F:MISC_F + N_HEADS_A].set(b_forget[j])
            bound_a = _logit_bound(q_norm_a[j], k_norm_a[j])
            bound_b = _logit_bound(q_norm_b[j], k_norm_b[j])
            shifts = jnp.stack([jnp.full((LANES,), LOGIT_CEIL_LOG2 - bound_a, F32),
                                jnp.full((LANES,), LOGIT_CEIL_LOG2 - bound_b, F32)])
            qa, ka, va_t, qb, kb, vb_t, qi, ki, wi_t, cs = _prep_even(
                proj, cos, sin, _gain_tile(q_norm_a[j], qscale), _gain_tile(k_norm_a[j]),
                _gain_tile(q_norm_b[j], qscale), _gain_tile(k_norm_b[j]), bf_pad, shifts, bmat, tri,
                tm=tm_prep)
            first_a = _first_live_tile(cs[:, MISC_F:MISC_F + N_HEADS_A], t_attn)
            out_a = _attention_any(qa, ka, va_t, None, bound_a, first_a, t=t_attn)
            mask = _indexer(qi, ki, wi_t, tq=min(2 * LANES, s), tk=t_attn, topk=topk, group=16)
            out_b = _attention_any(qb, kb, vb_t, mask, bound_b, jnp.zeros_like(first_a), t=t_attn)
            half = N_HEADS_A * HEAD_DIM
            xs = _matmul_res([(out_a, w_o[:half]), (out_b, w_o[half:])], xs, tm=tm)
        else:
            proj = _rms_matmul(xs, g_mix, w_in_odd[j].astype(BF16), tm=tm, tn=D_MODEL)
            q, k, v = _prep_odd(proj, cos, sin, _gain_tile(q_norm_c[j], qscale), _gain_tile(k_norm_c[j]),
                                bmat, tm=tm_prep)
            xs = _matmul_res([(_dilated(q, k, v), w_o)], xs, tm=tm)
        act = _ffn_act(xs, norm_ffn[i][None, :], w_gate[i].astype(BF16), w_up[i].astype(BF16),
                       conv_w[i], conv_b[i][None, :], tm=tm, tn=D_FF // 2)
        xs = _matmul_res([(act, w_down[i].astype(BF16))], xs, tm=tm)
        xs = _ple(xs, p[i, 0], norm_ple[i][None, :], w_ple_gate[i].astype(BF16),
                  w_ple_proj[i].astype(BF16), tm=tm)
    return xs[None]
```

```python
import functools

import numpy as np
import jax
import jax.numpy as jnp
from jax import lax
from jax.experimental import pallas as pl
from jax.experimental.pallas import tpu as pltpu

F32 = jnp.float32
BF16 = jnp.bfloat16

D_MODEL = 1024
HEAD_DIM = 64
N_HEADS = D_MODEL // HEAD_DIM
N_HEADS_A = N_HEADS // 2
N_HEADS_B = N_HEADS - N_HEADS_A
IDX_HEADS = 16
IDX_DIM = 64
TOPK_MAX = 256
DIL_PATTERNS = ((128, 1), (512, 4), (2048, 16))
Q_BLOCK = 128
D_FF = 2816
P_DIM = 256
ROPE_THETA = 10000.0
EPS = 1e-6
DEPTH = 4

LANES = 128
MXU_WIDTH = 256
LOG2E = 1.4426950408889634
NEG_INF = float("-inf")
INT_MIN = -(2 ** 31)
VMEM_LIMIT_BYTES = 56 * 1024 * 1024
LOGIT_CEIL_LOG2 = 40.0
LOGIT_BOUND_MAX = 80.0
LOGIT_DEAD = LOGIT_CEIL_LOG2 + 160.0

EV_QA, EV_KA, EV_VA, EV_QB, EV_KB, EV_VB, EV_QI, EV_MISC = 0, 512, 1024, 1536, 2048, 2560, 3072, 4096
EV_WIDTH = 4224
MISC_F = 64
MISC_W = 72

_NT = (((1,), (1,)), ((), ()))


def _params(n_axes):
    return pltpu.CompilerParams(dimension_semantics=("arbitrary",) * n_axes,
                                vmem_limit_bytes=VMEM_LIMIT_BYTES)


def _rmsnorm_rows(x, g):
    ms = jnp.mean(x * x, axis=-1, keepdims=True)
    return x * lax.rsqrt(ms + EPS) * g


def _split3(x):
    hi = x.astype(BF16)
    r = x - hi.astype(F32)
    mid = r.astype(BF16)
    lo = (r - mid.astype(F32)).astype(BF16)
    return hi, mid, lo


def _rms_matmul_kernel(x_ref, g_ref, w_ref, o_ref):
    h = _rmsnorm_rows(x_ref[...], g_ref[...]).astype(BF16)
    o_ref[...] = jnp.dot(h, w_ref[...], preferred_element_type=F32)


def _rms_matmul(x, g, w, *, tm, tn):
    s, d = x.shape
    n = w.shape[1]
    return pl.pallas_call(
        _rms_matmul_kernel,
        grid=(n // tn, s // tm),
        in_specs=[pl.BlockSpec((tm, d), lambda j, i: (i, 0)),
                  pl.BlockSpec((1, d), lambda j, i: (0, 0)),
                  pl.BlockSpec((d, tn), lambda j, i: (0, j))],
        out_specs=pl.BlockSpec((tm, tn), lambda j, i: (i, j)),
        out_shape=jax.ShapeDtypeStruct((s, n), F32),
        compiler_params=_params(2),
        name="rms_matmul",
    )(x, g, w)


def _head_rmsnorm(x, gain, bmat):
    sq = x * x
    hi = sq.astype(BF16)
    lo = (sq - hi.astype(F32)).astype(BF16)
    ms = (jnp.dot(hi, bmat, preferred_element_type=F32)
          + jnp.dot(lo, bmat, preferred_element_type=F32))
    return x * lax.rsqrt(ms + EPS) * gain


def _rope(x, cos, sin_signed, low_half):
    rot = jnp.where(low_half, pltpu.roll(x, LANES - 32, 1), pltpu.roll(x, 32, 1))
    return x * cos + rot * sin_signed


def _log_sigmoid(z):
    return jnp.minimum(z, 0.0) - jnp.log1p(jnp.exp(-jnp.abs(z)))


def _prep_even_kernel(proj_ref, cos_ref, sin_ref, gqa_ref, gka_ref, gqb_ref, gkb_ref, bf_ref,
                      shift_ref, bmat_ref, tri_ref,
                      qa_ref, ka_ref, va_ref, qb_ref, kb_ref, vb_ref, qi_ref, ki_ref, wi_ref, cs_ref, carry):
    tm = proj_ref.shape[0]
    shift_a = shift_ref[0:1, :]
    shift_b = shift_ref[1:2, :]

    @pl.when(pl.program_id(0) == 0)
    def _():
        carry[...] = jnp.zeros_like(carry)

    cos = cos_ref[...]
    sin = sin_ref[...]
    bmat = bmat_ref[...]
    lane = lax.broadcasted_iota(jnp.int32, (tm, LANES), 1)
    low_half = (lane & 32) == 0
    head0 = lane < HEAD_DIM

    misc = proj_ref[:, EV_MISC:EV_MISC + LANES]

    logf = _log_sigmoid(misc + bf_ref[...])
    logf = jnp.where((lane >= MISC_F) & (lane < MISC_F + N_HEADS_A), logf, 0.0)
    tri = tri_ref[...]
    f_hi, f_mid, f_lo = _split3(logf)
    c = (jnp.dot(tri, f_hi, preferred_element_type=F32)
         + jnp.dot(tri, f_mid, preferred_element_type=F32)
         + jnp.dot(tri, f_lo, preferred_element_type=F32)) + carry[0:1, :]
    carry[0:1, :] = c[tm - 1:tm, :]
    cs_ref[...] = c * LOG2E
    c_hi, c_mid, c_lo = (t.astype(F32) for t in _split3(c * LOG2E))

    def place(block, h):
        return block if h % 2 == 0 else pltpu.roll(block, HEAD_DIM, 1)

    for hp in range(N_HEADS_A // 2):
        cols = slice(hp * LANES, (hp + 1) * LANES)
        qa = _head_rmsnorm(proj_ref[:, EV_QA + hp * LANES:EV_QA + (hp + 1) * LANES], gqa_ref[...], bmat)
        ka = _head_rmsnorm(proj_ref[:, EV_KA + hp * LANES:EV_KA + (hp + 1) * LANES], gka_ref[...], bmat)
        va = proj_ref[:, EV_VA + hp * LANES:EV_VA + (hp + 1) * LANES]
        qb = _rope(_head_rmsnorm(proj_ref[:, EV_QB + hp * LANES:EV_QB + (hp + 1) * LANES], gqb_ref[...], bmat),
                   cos, sin, low_half)
        kb = _rope(_head_rmsnorm(proj_ref[:, EV_KB + hp * LANES:EV_KB + (hp + 1) * LANES], gkb_ref[...], bmat),
                   cos, sin, low_half)
        vb = proj_ref[:, EV_VB + hp * LANES:EV_VB + (hp + 1) * LANES]
        del cols
        for sub in range(2):
            h = 2 * hp + sub
            src = MISC_F + h
            q_aug = jnp.where(
                head0, place(qa, h),
                jnp.where(lane == 64, pltpu.roll(c_hi, (64 - src) % LANES, 1),
                          jnp.where(lane == 65, pltpu.roll(c_mid, (65 - src) % LANES, 1),
                                    jnp.where(lane == 66, pltpu.roll(c_lo, (66 - src) % LANES, 1),
                                              jnp.where(lane < 70, 1.0,
                                                        jnp.where(lane == 70, shift_a, 0.0))))))
            k_aug = jnp.where(
                head0, place(ka, h),
                jnp.where(lane < 67, 1.0,
                          jnp.where(lane == 67, -pltpu.roll(c_hi, (67 - src) % LANES, 1),
                                    jnp.where(lane == 68, -pltpu.roll(c_mid, (68 - src) % LANES, 1),
                                              jnp.where(lane == 69, -pltpu.roll(c_lo, (69 - src) % LANES, 1),
                                                        jnp.where(lane == 70, 1.0, 0.0))))))
            ones_col = jnp.where(lane == HEAD_DIM, 1.0, 0.0)
            qa_ref[h] = q_aug.astype(BF16)
            ka_ref[h] = k_aug.astype(BF16)
            va_ref[h] = jnp.where(head0, place(va, h), ones_col).T.astype(BF16)
            qb_ref[h] = jnp.where(head0, place(qb, h), jnp.where(lane == HEAD_DIM, shift_b, 0.0)).astype(BF16)
            kb_ref[h] = jnp.where(head0, place(kb, h), ones_col).astype(BF16)
            vb_ref[h] = jnp.where(head0, place(vb, h), ones_col).T.astype(BF16)

    for blk in range(IDX_HEADS * IDX_DIM // LANES):
        cols = slice(EV_QI + blk * LANES, EV_QI + (blk + 1) * LANES)
        qi = _rope(proj_ref[:, cols], cos, sin, low_half) * (IDX_DIM ** -0.5)
        qi_ref[2 * blk] = qi[:, 0:IDX_DIM].astype(BF16)
        qi_ref[2 * blk + 1] = qi[:, IDX_DIM:2 * IDX_DIM].astype(BF16)
    ki_ref[...] = _rope(misc, cos, sin, low_half)[:, 0:IDX_DIM].astype(BF16)
    wi_ref[...] = (misc * (IDX_HEADS ** -0.5)).T


def _prep_even(proj, cos, sin, gqa, gka, gqb, gkb, bf_pad, shifts, bmat, tri, *, tm):
    s = proj.shape[0]
    row = lambda w: pl.BlockSpec((tm, w), lambda i: (i, 0))
    const = lambda a: pl.BlockSpec(a.shape, lambda i: (0, 0))
    heads = pl.BlockSpec((N_HEADS_A, tm, LANES), lambda i: (0, i, 0))
    heads_shape = jax.ShapeDtypeStruct((N_HEADS_A, s, LANES), BF16)
    heads_t = pl.BlockSpec((N_HEADS_A, LANES, tm), lambda i: (0, 0, i))
    heads_t_shape = jax.ShapeDtypeStruct((N_HEADS_A, LANES, s), BF16)
    return pl.pallas_call(
        _prep_even_kernel,
        grid=(s // tm,),
        in_specs=[row(EV_WIDTH), row(LANES), row(LANES), const(gqa), const(gka), const(gqb), const(gkb),
                  const(bf_pad), const(shifts), const(bmat), const(tri)],
        out_specs=[heads, heads, heads_t, heads, heads, heads_t,
                   pl.BlockSpec((IDX_HEADS, tm, IDX_DIM), lambda i: (0, i, 0)),
                   row(IDX_DIM),
                   pl.BlockSpec((LANES, tm), lambda i: (0, i)),
                   row(LANES)],
        out_shape=[heads_shape, heads_shape, heads_t_shape, heads_shape, heads_shape, heads_t_shape,
                   jax.ShapeDtypeStruct((IDX_HEADS, s, IDX_DIM), BF16),
                   jax.ShapeDtypeStruct((s, IDX_DIM), BF16),
                   jax.ShapeDtypeStruct((LANES, s), F32),
                   jax.ShapeDtypeStruct((s, LANES), F32)],
        scratch_shapes=[pltpu.VMEM((8, LANES), F32)],
        compiler_params=_params(1),
        name="prep_even",
    )(proj, cos, sin, gqa, gka, gqb, gkb, bf_pad, shifts, bmat, tri)


def _float_key(x):
    bits = pltpu.bitcast(x, jnp.int32)
    return bits ^ ((bits >> 31) & 0x7FFFFFFF)


def _key_float(k):
    return pltpu.bitcast(k ^ ((k >> 31) & 0x7FFFFFFF), F32)


def _kth_search(count_fn, lo_k, hi_k, c_lo, done, kf, *, fixed_steps=None, steps_per_check=1,
                stop_on_count=True):
    def finished(lo, hi, c, d):
        d = d | (lo + 1 >= hi)
        return jnp.where(d | (c == kf) if stop_on_count else d, 1.0, 0.0)

    def halve(lo, hi, c, d):
        live = d == 0.0
        cand = (lo >> 1) + (hi >> 1) + (lo & hi & 1)
        n = count_fn(_key_float(cand))
        take = (n >= kf) & live
        drop = (n < kf) & live
        lo = jnp.where(take, cand, lo)
        c = jnp.where(take, n, c)
        hi = jnp.where(drop, cand, hi)
        return lo, hi, c, finished(lo, hi, c, d > 0.0)

    d0 = finished(lo_k, hi_k, c_lo, done)
    if fixed_steps is not None:
        lo_k, _, c_lo, _ = lax.fori_loop(0, fixed_steps, lambda _, st: halve(*st), (lo_k, hi_k, c_lo, d0))
        return lo_k, c_lo

    def body(state):
        st = state[:4]
        for _ in range(steps_per_check):
            st = halve(*st)
        return (*st, jnp.max(1.0 - st[3]))

    lo_k, _, c_lo, _, _ = lax.while_loop(lambda state: state[4] > 0.0, body,
                                         (lo_k, hi_k, c_lo, d0, jnp.max(1.0 - d0)))
    return lo_k, c_lo


def _indexer_kernel(qi_ref, ki_ref, wi_ref, bias_ref, sc_scr, g_scr, *, tq, tk, topk, group):
    i = pl.program_id(0)
    s_keys = sc_scr.shape[0]
    n_chunks = s_keys // tk
    nk = ((i + 1) * tq + tk - 1) // tk
    sub = 8
    g_rows = tk // group
    heads_per_dot = max(1, MXU_WIDTH // tq)
    kf = float(topk)
    lowest = float(np.finfo(np.float32).min)
    qpos = lax.broadcasted_iota(jnp.int32, (1, tq), 1) + i * tq
    w = wi_ref[...]

    g_scr[...] = jnp.full(g_scr.shape, NEG_INF, F32)

    def score_chunk(c, gmax):
        ks = pl.multiple_of(c * tk, tk)
        ki = ki_ref[pl.ds(ks, tk), :]
        acc = jnp.zeros((tk, tq), F32)
        for first_head in range(0, IDX_HEADS, heads_per_dot):
            q_rows = [qi_ref[first_head + u] for u in range(heads_per_dot)]
            q_cat = q_rows[0] if heads_per_dot == 1 else jnp.concatenate(q_rows, axis=0)
            logits = lax.dot_general(ki, q_cat, _NT, preferred_element_type=F32)
            for u in range(heads_per_dot):
                w_h = w[MISC_W + first_head + u:MISC_W + first_head + u + 1, :]
                acc = acc + w_h * jnp.maximum(logits[:, u * tq:(u + 1) * tq], 0.0)
        kpos = lax.broadcasted_iota(jnp.int32, (tk, tq), 0) + c * tk
        sc = jnp.where(kpos <= qpos, acc, NEG_INF)
        sc_scr[pl.ds(ks, tk), :] = sc
        g = jnp.max(sc.reshape(tk // (group * sub), group, sub, tq), axis=1).reshape(g_rows, tq)
        g_scr[pl.ds(pl.multiple_of(c * g_rows, g_rows), g_rows), :] = g
        return jnp.maximum(gmax, jnp.max(g, axis=0, keepdims=True))

    gmax = lax.fori_loop(0, nk, score_chunk, jnp.full((1, tq), NEG_INF, F32))

    def count_rows(ref, rows_per_step, n_steps, cand):
        ways = min(8, rows_per_step // sub)

        def step(c, cnt):
            r0 = pl.multiple_of(c * rows_per_step, rows_per_step)
            ones = jnp.where(ref[pl.ds(r0, rows_per_step), :] >= cand, 1.0, 0.0)
            return cnt + jnp.sum(ones.reshape(rows_per_step // (ways * sub), ways, sub, tq), axis=0)

        cnt = lax.fori_loop(0, n_steps, step, jnp.zeros((ways, sub, tq), F32))
        return jnp.sum(jnp.sum(cnt, axis=0), axis=0, keepdims=True)

    g_piece = min(256, g_scr.shape[0])
    count_groups = lambda cand: count_rows(g_scr, g_piece, (nk * g_rows + g_piece - 1) // g_piece, cand)
    count_all = lambda cand: count_rows(sc_scr, tk, nk, cand)

    lo0 = _float_key(jnp.full((1, tq), lowest, F32))
    hi0 = _float_key(gmax) + 1
    few = qpos < topk
    c_groups = count_groups(jnp.full((1, tq), lowest, F32))
    u_k, _ = _kth_search(count_groups, lo0, hi0, c_groups, few | (c_groups < kf), kf, fixed_steps=32)
    u = _key_float(u_k)
    c_u = count_all(u)
    n_max = count_groups(u)
    rank = kf - (c_u - n_max)
    has_y = jnp.logical_not(few) & (rank >= 1.0) & (rank <= n_max)
    y_k, _ = _kth_search(count_groups, u_k, hi0, n_max, jnp.logical_not(has_y), rank, fixed_steps=32,
                         stop_on_count=False)
    hi1 = jnp.where(has_y, y_k + 1, hi0)
    t_k, _ = _kth_search(count_all, u_k, hi1, c_u, few, kf, steps_per_check=2)
    thr = jnp.where(few, lowest, _key_float(t_k))

    def bias_chunk(c, carry):
        ks = pl.multiple_of(c * tk, tk)
        bias_ref[pl.ds(ks, tk), :] = jnp.where(sc_scr[pl.ds(ks, tk), :] >= thr, 1.0, 0.0).astype(BF16)
        return carry

    lax.fori_loop(0, nk, bias_chunk, 0)

    def fill_chunk(c, carry):
        bias_ref[pl.ds(pl.multiple_of(c * tk, tk), tk), :] = jnp.zeros((tk, tq), BF16)
        return carry

    lax.fori_loop(nk, n_chunks, fill_chunk, 0)


def _indexer(qi, ki, wi_t, *, tq, tk, topk, group):
    s = ki.shape[0]
    return pl.pallas_call(
        functools.partial(_indexer_kernel, tq=tq, tk=tk, topk=topk, group=group),
        grid=(s // tq,),
        in_specs=[pl.BlockSpec((IDX_HEADS, tq, IDX_DIM), lambda i: (0, i, 0)),
                  pl.BlockSpec((s, IDX_DIM), lambda i: (0, 0)),
                  pl.BlockSpec((LANES, tq), lambda i: (0, i))],
        out_specs=pl.BlockSpec((None, s, tq), lambda i: (i, 0, 0)),
        out_shape=jax.ShapeDtypeStruct((s // tq, s, tq), BF16),
        scratch_shapes=[pltpu.VMEM((s, tq), F32), pltpu.VMEM((s // group, tq), F32)],
        compiler_params=_params(1),
        name="dsa_indexer",
    )(qi, ki, wi_t)


def _attn_kernel(*refs, has_mask, bounded):
    first_ref, q_ref, k_ref, vt_ref = refs[:4]
    refs = refs[1:]
    mask_ref = refs[3] if has_mask else None
    o_ref = refs[3 + has_mask]
    scratch = refs[4 + has_mask:]
    acc_scr = scratch[-1]
    m_scr = None if bounded else scratch[0]
    n_heads, t, _ = q_ref.shape
    i = pl.program_id(0)
    j = pl.program_id(1)

    @pl.when(j == 0)
    def _():
        acc_scr[...] = jnp.zeros(acc_scr.shape, F32)
        if not bounded:
            m_scr[...] = jnp.full(m_scr.shape, NEG_INF, F32)

    def step(diagonal):
        if has_mask:
            mask = jnp.concatenate([mask_ref[b] for b in range(mask_ref.shape[0])], axis=1)
            if not bounded:
                selected = mask.astype(F32) > 0.0
        if diagonal:
            causal = (lax.broadcasted_iota(jnp.int32, (t, t), 0)
                      <= lax.broadcasted_iota(jnp.int32, (t, t), 1))

        def logits(h):
            return lax.dot_general(k_ref[h], q_ref[h], _NT, preferred_element_type=F32)

        s_next = logits(0)
        for h in range(n_heads):
            s = s_next
            if h + 1 < n_heads:
                s_next = logits(h + 1)
            if diagonal:
                s = jnp.where(causal, s, NEG_INF)
            if bounded:
                p = jnp.exp2(s).astype(BF16)
                if has_mask:
                    p = p * mask
                acc_scr[h] = acc_scr[h] + jnp.dot(vt_ref[h], p, preferred_element_type=F32)
            else:
                if has_mask:
                    s = jnp.where(selected, s, NEG_INF)
                m_old = m_scr[h][0:1, :]
                m_new = jnp.maximum(m_old, jnp.max(s, axis=0, keepdims=True))
                m_safe = jnp.where(m_new == NEG_INF, 0.0, m_new)
                p = jnp.exp2(s - m_safe)
                alpha = jnp.exp2(m_old - m_safe)
                acc_scr[h] = alpha * acc_scr[h] + jnp.dot(vt_ref[h], p.astype(BF16), preferred_element_type=F32)
                m_scr[h] = jnp.broadcast_to(m_new, m_scr.shape[1:])

    @pl.when((j < i) & (j >= first_ref[i]))
    def _():
        step(False)

    @pl.when(j == i)
    def _():
        step(True)
        for hp in range(n_heads // 2):
            halves = []
            for h in (2 * hp, 2 * hp + 1):
                a = acc_scr[h]
                halves.append(a[0:HEAD_DIM, :] / a[HEAD_DIM:HEAD_DIM + 1, :])
            o_ref[:, hp * LANES:(hp + 1) * LANES] = jnp.concatenate(halves, axis=0).T.astype(o_ref.dtype)


def _attention(q, k, vt, mask, first, *, t, bounded):
    n_heads, s, _ = q.shape
    sub = 8
    key_tile = lambda i, j, first_ref: jnp.clip(j, first_ref[i], i)
    in_specs = [pl.BlockSpec((n_heads, t, LANES), lambda i, j, f: (0, i, 0)),
                pl.BlockSpec((n_heads, t, LANES), lambda i, j, f: (0, key_tile(i, j, f), 0)),
                pl.BlockSpec((n_heads, LANES, t), lambda i, j, f: (0, 0, key_tile(i, j, f)))]
    args = [q, k, vt]
    if mask is not None:
        tq_mask = mask.shape[2]
        in_specs.append(pl.BlockSpec((t // tq_mask, t, tq_mask), lambda i, j, f: (i, key_tile(i, j, f), 0)))
        args.append(mask)
    scratch = [pltpu.VMEM((n_heads, LANES, t), F32)]
    if not bounded:
        scratch.insert(0, pltpu.VMEM((n_heads, sub, t), F32))
    name = ("dsa_attention" if mask is not None else "fox_attention") + ("" if bounded else "_online")
    return pl.pallas_call(
        functools.partial(_attn_kernel, has_mask=mask is not None, bounded=bounded),
        grid_spec=pltpu.PrefetchScalarGridSpec(
            num_scalar_prefetch=1,
            grid=(s // t, s // t),
            in_specs=in_specs,
            out_specs=pl.BlockSpec((t, n_heads * HEAD_DIM), lambda i, j, f: (i, 0)),
            scratch_shapes=scratch),
        out_shape=jax.ShapeDtypeStruct((s, n_heads * HEAD_DIM), BF16),
        compiler_params=_params(2),
        name=name,
    )(first, *args)


def _attention_any(q, k, vt, mask, logit_bound, first, *, t):
    operands = (q, k, vt, first) if mask is None else (q, k, vt, first, mask)
    m = lambda ops: ops[4] if mask is not None else None
    return lax.cond(logit_bound <= LOGIT_BOUND_MAX,
                    lambda *ops: _attention(ops[0], ops[1], ops[2], m(ops), ops[3], t=t, bounded=True),
                    lambda *ops: _attention(ops[0], ops[1], ops[2], m(ops), jnp.zeros_like(ops[3]), t=t,
                                            bounded=False),
                    *operands)


def _first_live_tile(cs, t):
    top = cs[0::t]
    bottom = cs[t - 1::t]
    live = jnp.any(top[:, None, :] - bottom[None, :, :] > -LOGIT_DEAD, axis=-1)
    return jnp.argmax(live, axis=1).astype(jnp.int32)


def _matmul_res_kernel(*refs, n_pairs):
    res_ref = refs[2 * n_pairs]
    o_ref = refs[2 * n_pairs + 1]
    acc = res_ref[...]
    for p in range(n_pairs):
        acc = acc + jnp.dot(refs[2 * p][...], refs[2 * p + 1][...], preferred_element_type=F32)
    o_ref[...] = acc


def _matmul_res(pairs, res, *, tm):
    s, n = res.shape
    in_specs, args = [], []
    for a, w in pairs:
        in_specs += [pl.BlockSpec((tm, a.shape[1]), lambda i: (i, 0)),
                     pl.BlockSpec(w.shape, lambda i: (0, 0))]
        args += [a, w]
    in_specs.append(pl.BlockSpec((tm, n), lambda i: (i, 0)))
    args.append(res)
    return pl.pallas_call(
        functools.partial(_matmul_res_kernel, n_pairs=len(pairs)),
        grid=(s // tm,),
        in_specs=in_specs,
        out_specs=pl.BlockSpec((tm, n), lambda i: (i, 0)),
        out_shape=jax.ShapeDtypeStruct((s, n), F32),
        compiler_params=_params(1),
        name="matmul_residual",
    )(*args)


def _prep_odd_kernel(proj_ref, cos_ref, sin_ref, gq_ref, gk_ref, bmat_ref, q_ref, k_ref, v_ref):
    tm = proj_ref.shape[0]
    cos = cos_ref[...]
    sin = sin_ref[...]
    bmat = bmat_ref[...]
    lane = lax.broadcasted_iota(jnp.int32, (tm, LANES), 1)
    low_half = (lane & 32) == 0
    for blk in range(D_MODEL // LANES):
        cols = slice(blk * LANES, (blk + 1) * LANES)
        q = proj_ref[:, blk * LANES:(blk + 1) * LANES]
        k = proj_ref[:, D_MODEL + blk * LANES:D_MODEL + (blk + 1) * LANES]
        q_ref[:, cols] = _rope(_head_rmsnorm(q, gq_ref[...], bmat), cos, sin, low_half)
        k_ref[:, cols] = _rope(_head_rmsnorm(k, gk_ref[...], bmat), cos, sin, low_half)
        v_ref[:, cols] = proj_ref[:, 2 * D_MODEL + blk * LANES:2 * D_MODEL + (blk + 1) * LANES]


def _prep_odd(proj, cos, sin, gq, gk, bmat, *, tm):
    s = proj.shape[0]
    row = lambda w: pl.BlockSpec((tm, w), lambda i: (i, 0))
    const = lambda a: pl.BlockSpec(a.shape, lambda i: (0, 0))
    shape = jax.ShapeDtypeStruct((s, D_MODEL), F32)
    return pl.pallas_call(
        _prep_odd_kernel,
        grid=(s // tm,),
        in_specs=[row(3 * D_MODEL), row(LANES), row(LANES), const(gq), const(gk), const(bmat)],
        out_specs=[row(D_MODEL)] * 3,
        out_shape=[shape] * 3,
        compiler_params=_params(1),
        name="prep_odd",
    )(proj, cos, sin, gq, gk, bmat)


def _band_head_pair(q2, k2, v2, ok, first, first2, shift):
    zero = jnp.zeros((), BF16)
    one = jnp.ones((), BF16)
    res, mx = [], []
    for sub in range(2):
        mine = first if sub == 0 else jnp.logical_not(first)
        mine2 = first2 if sub == 0 else jnp.logical_not(first2)
        s = lax.dot_general(jnp.where(mine, q2, zero), k2, _NT, preferred_element_type=F32)
        if shift is None:
            s = jnp.where(ok, s, NEG_INF)
            m = jnp.max(s, axis=-1, keepdims=True)
            p = jnp.exp2(s - m).astype(BF16)
            mx.append(m)
        else:
            p = jnp.exp2(jnp.where(ok, s + shift, NEG_INF)).astype(BF16)
        res.append(jnp.dot(p, jnp.where(mine2, v2, one), preferred_element_type=F32))
    num = jnp.where(first, res[0], res[1])
    den_sw = jnp.where(first, res[1], res[0])
    den = pltpu.roll(den_sw, HEAD_DIM, 1)
    if shift is not None:
        return num, den
    return num / den, jnp.where(first, mx[0], mx[1]) + jnp.log2(den)


def _dilated_kernel(shift_ref, q_ref, kc_ref, kp_ref, vc_ref, vp_ref, o_ref, o_scr, l_scr, *, bounded):
    shift = shift_ref[0] if bounded else None
    win = q_ref.shape[0]
    t = Q_BLOCK
    qi = lax.broadcasted_iota(jnp.int32, (t, t), 0)
    kj = lax.broadcasted_iota(jnp.int32, (t, t), 1)
    ok_inner = jnp.concatenate([kj >= qi, kj <= qi], axis=1)
    ok_first = jnp.concatenate([(kj >= qi) & (pl.program_id(0) > 0), kj <= qi], axis=1)
    first = lax.broadcasted_iota(jnp.int32, (t, LANES), 1) < HEAD_DIM
    first2 = lax.broadcasted_iota(jnp.int32, (2 * t, LANES), 1) < HEAD_DIM

    for p, (_, dil) in enumerate(DIL_PATTERNS):
        span = dil * t
        n_blocks = win // span
        for r in range(dil):
            for cb in range(n_blocks):
                rows = pl.ds(r + cb * span, t, stride=dil) if dil > 1 else pl.ds(cb * t, t)
                if cb > 0:
                    prows = pl.ds(r + (cb - 1) * span, t, stride=dil) if dil > 1 else pl.ds((cb - 1) * t, t)
                    k_prev, v_prev, ok = kc_ref[prows, :], vc_ref[prows, :], ok_inner
                else:
                    prows = (pl.ds(r + (n_blocks - 1) * span, t, stride=dil) if dil > 1
                             else pl.ds((n_blocks - 1) * t, t))
                    k_prev, v_prev, ok = kp_ref[prows, :], vp_ref[prows, :], ok_first
                k2 = jnp.concatenate([k_prev, kc_ref[rows, :]], axis=0).astype(BF16)
                v2 = jnp.concatenate([v_prev, vc_ref[rows, :]], axis=0).astype(BF16)
                o2, l2 = _band_head_pair(q_ref[rows, :].astype(BF16), k2, v2, ok, first, first2, shift)
                o_scr[p, rows, :] = o2
                l_scr[p, rows, :] = l2

    if bounded:
        merged = (o_scr[0] + o_scr[1] + o_scr[2]) / (l_scr[0] + l_scr[1] + l_scr[2])
    else:
        l1, l2, l3 = l_scr[0], l_scr[1], l_scr[2]
        m = jnp.maximum(jnp.maximum(l1, l2), l3)
        w1, w2, w3 = jnp.exp2(l1 - m), jnp.exp2(l2 - m), jnp.exp2(l3 - m)
        merged = (w1 * o_scr[0] + w2 * o_scr[1] + w3 * o_scr[2]) / (w1 + w2 + w3)
    o_ref[...] = merged.astype(o_ref.dtype)


def _dilated(q, k, v, shift, *, bounded):
    s = q.shape[0]
    win = max(dil for _, dil in DIL_PATTERNS) * Q_BLOCK
    assert s % win == 0
    cur = pl.BlockSpec((win, LANES), lambda c, j: (c, j))
    prev = pl.BlockSpec((win, LANES), lambda c, j: (jnp.maximum(c - 1, 0), j))
    return pl.pallas_call(
        functools.partial(_dilated_kernel, bounded=bounded),
        grid=(s // win, D_MODEL // LANES),
        in_specs=[pl.BlockSpec(memory_space=pltpu.SMEM), cur, cur, prev, cur, prev],
        out_specs=cur,
        out_shape=jax.ShapeDtypeStruct((s, D_MODEL), BF16),
        scratch_shapes=[pltpu.VMEM((len(DIL_PATTERNS), win, LANES), F32),
                        pltpu.VMEM((len(DIL_PATTERNS), win, LANES), F32)],
        compiler_params=_params(2),
        name="dilated_attention" if bounded else "dilated_attention_online",
    )(shift, q, k, k, v, v)


def _dilated_any(q, k, v, logit_bound):
    shift = jnp.reshape(LOGIT_CEIL_LOG2 - logit_bound, (1,)).astype(F32)
    return lax.cond(logit_bound <= LOGIT_BOUND_MAX,
                    lambda *ops: _dilated(*ops, bounded=True),
                    lambda *ops: _dilated(*ops, bounded=False),
                    q, k, v, shift)


def _gelu_tanh(x):
    return 0.5 * x * (1.0 + jnp.tanh(np.sqrt(2.0 / np.pi).astype(np.float32) * (x + 0.044715 * (x * x * x))))


def _ffn_act_kernel(x_ref, halo_ref, g_ref, wg_ref, wu_ref, cw_ref, cb_ref, o_ref):
    i = pl.program_id(1)
    tm = x_ref.shape[0]
    h = _rmsnorm_rows(x_ref[...], g_ref[...]).astype(BF16)
    hh = _rmsnorm_rows(halo_ref[...], g_ref[...]).astype(BF16)
    wg = wg_ref[...]
    gate = jnp.dot(h, wg, preferred_element_type=F32)
    gate_halo = jnp.where(i > 0, jnp.dot(hh, wg, preferred_element_type=F32), 0.0)
    up = jnp.dot(h, wu_ref[...], preferred_element_type=F32)
    row = lax.broadcasted_iota(jnp.int32, gate.shape, 0)
    g_m1 = jnp.where(row == 0, gate_halo[7:8, :], pltpu.roll(gate, 1, 0))
    g_m2 = jnp.where(row == 0, gate_halo[6:7, :],
                     jnp.where(row == 1, gate_halo[7:8, :], pltpu.roll(gate, 2, 0)))
    cw = cw_ref[...]
    conv = cw[2:3, :] * gate + cw[1:2, :] * g_m1 + cw[0:1, :] * g_m2 + cb_ref[...]
    o_ref[...] = (_gelu_tanh(conv) * up).astype(o_ref.dtype)
    del tm


def _ffn_act(x, g, wg, wu, cw, cb, *, tm, tn):
    s, d = x.shape
    n = wg.shape[1]
    halo_rows = 8
    return pl.pallas_call(
        _ffn_act_kernel,
        grid=(n // tn, s // tm),
        in_specs=[pl.BlockSpec((tm, d), lambda j, i: (i, 0)),
                  pl.BlockSpec((halo_rows, d), lambda j, i: (jnp.maximum(i * (tm // halo_rows) - 1, 0), 0)),
                  pl.BlockSpec((1, d), lambda j, i: (0, 0)),
                  pl.BlockSpec((d, tn), lambda j, i: (0, j)),
                  pl.BlockSpec((d, tn), lambda j, i: (0, j)),
                  pl.BlockSpec((cw.shape[0], tn), lambda j, i: (0, j)),
                  pl.BlockSpec((1, tn), lambda j, i: (0, j))],
        out_specs=pl.BlockSpec((tm, tn), lambda j, i: (i, j)),
        out_shape=jax.ShapeDtypeStruct((s, n), BF16),
        compiler_params=_params(2),
        name="ffn_act",
    )(x, x, g, wg, wu, cw, cb)


def _ple_kernel(x_ref, p_ref, g_ref, wg_ref, wp_ref, o_ref):
    x = x_ref[...]
    z = jnp.dot(_rmsnorm_rows(x, g_ref[...]).astype(BF16), wg_ref[...], preferred_element_type=F32)
    gate = 1.0 / (1.0 + jnp.exp(-z))
    proj = jnp.dot(p_ref[...].astype(BF16), wp_ref[...], preferred_element_type=F32)
    o_ref[...] = x + gate * proj


def _ple(x, p, g, wg, wp, *, tm):
    s, d = x.shape
    return pl.pallas_call(
        _ple_kernel,
        grid=(s // tm,),
        in_specs=[pl.BlockSpec((tm, d), lambda i: (i, 0)),
                  pl.BlockSpec((tm, p.shape[1]), lambda i: (i, 0)),
                  pl.BlockSpec((1, d), lambda i: (0, 0)),
                  pl.BlockSpec(wg.shape, lambda i: (0, 0)),
                  pl.BlockSpec(wp.shape, lambda i: (0, 0))],
        out_specs=pl.BlockSpec((tm, d), lambda i: (i, 0)),
        out_shape=jax.ShapeDtypeStruct((s, d), F32),
        compiler_params=_params(1),
        name="ple",
    )(x, p, g, wg, wp)


def _rope_tables(s):
    half = HEAD_DIM // 2
    inv_freq = ROPE_THETA ** (-jnp.arange(half, dtype=F32) / half)
    ang = jnp.arange(s).astype(F32)[:, None] * inv_freq[None, :]
    cos, sin = jnp.cos(ang), jnp.sin(ang)
    return jnp.tile(cos, (1, 4)), jnp.tile(jnp.concatenate([-sin, sin], axis=1), (1, 2))


def _logit_bound(gq, gk):
    return (1.02 * HEAD_DIM * HEAD_DIM ** -0.5 * LOG2E) * jnp.max(jnp.abs(gq)) * jnp.max(jnp.abs(gk))


def _gain_tile(g, scale=1.0):
    return (jnp.tile(g.astype(F32), 2) * scale)[None, :]


def _even_w_in(w):
    hd = N_HEADS_A * HEAD_DIM
    cuts = np.cumsum([hd, hd, hd, N_HEADS_A, hd, hd, hd, IDX_HEADS * IDX_DIM, IDX_DIM, IDX_HEADS])[:-1]
    qa, ka, va, fa, qb, kb, vb, qi, ki, wi = jnp.split(w, [int(c) for c in cuts], axis=1)
    pad = jnp.zeros((w.shape[0], LANES - IDX_DIM - N_HEADS_A - IDX_HEADS), w.dtype)
    return jnp.concatenate([qa, ka, va, qb, kb, vb, qi, ki, fa, wi, pad], axis=1).astype(BF16)


def kernel(x, p, norm_mix, w_in_even, b_forget, q_norm_a, k_norm_a, q_norm_b, k_norm_b, w_in_odd, q_norm_c,
           k_norm_c, w_out, norm_ffn, w_gate, w_up, conv_w, conv_b, w_down, norm_ple, w_ple_gate, w_ple_proj):
    batch, s, d = x.shape
    assert batch == 1 and d == D_MODEL
    depth = norm_mix.shape[0]
    topk = min(TOPK_MAX, s // 4)
    t_attn = min(512, s)
    tm = min(512, s)
    tm_prep = min(256, s)
    qscale = HEAD_DIM ** -0.5 * LOG2E

    cos, sin = _rope_tables(s)
    lane = np.arange(LANES)
    bmat = jnp.asarray((lane[:, None] // HEAD_DIM == lane[None, :] // HEAD_DIM) / HEAD_DIM, BF16)
    tri = jnp.asarray(np.tril(np.ones((tm_prep, tm_prep))), BF16)

    xs = x[0]
    for i in range(depth):
        j = i // 2
        g_mix = norm_mix[i][None, :]
        w_o = w_out[i].astype(BF16)
        if i % 2 == 0:
            proj = _rms_matmul(xs, g_mix, _even_w_in(w_in_even[j]), tm=tm, tn=EV_WIDTH // 3)
            bf_pad = jnp.zeros((1, LANES), F32).at[0, MISC_F:MISC_F + N_HEADS_A].set(b_forget[j])
            bound_a = _logit_bound(q_norm_a[j], k_norm_a[j])
            bound_b = _logit_bound(q_norm_b[j], k_norm_b[j])
            shifts = jnp.stack([jnp.full((LANES,), LOGIT_CEIL_LOG2 - bound_a, F32),
                                jnp.full((LANES,), LOGIT_CEIL_LOG2 - bound_b, F32)])
            qa, ka, va_t, qb, kb, vb_t, qi, ki, wi_t, cs = _prep_even(
                proj, cos, sin, _gain_tile(q_norm_a[j], qscale), _gain_tile(k_norm_a[j]),
                _gain_tile(q_norm_b[j], qscale), _gain_tile(k_norm_b[j]), bf_pad, shifts, bmat, tri,
                tm=tm_prep)
            first_a = _first_live_tile(cs[:, MISC_F:MISC_F + N_HEADS_A], t_attn)
            out_a = _attention_any(qa, ka, va_t, None, bound_a, first_a, t=t_attn)
            mask = _indexer(qi, ki, wi_t, tq=min(2 * LANES, s), tk=t_attn, topk=topk, group=16)
            out_b = _attention_any(qb, kb, vb_t, mask, bound_b, jnp.zeros_like(first_a), t=t_attn)
            half = N_HEADS_A * HEAD_DIM
            xs = _matmul_res([(out_a, w_o[:half]), (out_b, w_o[half:])], xs, tm=tm)
        else:
            proj = _rms_matmul(xs, g_mix, w_in_odd[j].astype(BF16), tm=tm, tn=D_MODEL)
            q, k, v = _prep_odd(proj, cos, sin, _gain_tile(q_norm_c[j], qscale), _gain_tile(k_norm_c[j]),
                                bmat, tm=tm_prep)
            merged = _dilated_any(q, k, v, _logit_bound(q_norm_c[j], k_norm_c[j]))
            xs = _matmul_res([(merged, w_o)], xs, tm=tm)
        act = _ffn_act(xs, norm_ffn[i][None, :], w_gate[i].astype(BF16), w_up[i].astype(BF16),
                       conv_w[i], conv_b[i][None, :], tm=tm, tn=D_FF // 2)
        xs = _matmul_res([(act, w_down[i].astype(BF16))], xs, tm=tm)
        xs = _ple(xs, p[i, 0], norm_ple[i][None, :], w_ple_gate[i].astype(BF16),
                  w_ple_proj[i].astype(BF16), tm=tm)
    return xs[None]
```

```python
import functools

import numpy as np
import jax
import jax.numpy as jnp
from jax import lax
from jax.experimental import pallas as pl
from jax.experimental.pallas import tpu as pltpu

F32 = jnp.float32
BF16 = jnp.bfloat16

D_MODEL = 1024
HEAD_DIM = 64
N_HEADS = D_MODEL // HEAD_DIM
N_HEADS_A = N_HEADS // 2
N_HEADS_B = N_HEADS - N_HEADS_A
IDX_HEADS = 16
IDX_DIM = 64
TOPK_MAX = 256
DIL_PATTERNS = ((128, 1), (512, 4), (2048, 16))
Q_BLOCK = 128
D_FF = 2816
P_DIM = 256
ROPE_THETA = 10000.0
EPS = 1e-6
DEPTH = 4

LANES = 128
MXU_WIDTH = 256
LOG2E = 1.4426950408889634
NEG_INF = float("-inf")
INT_MIN = -(2 ** 31)
VMEM_LIMIT_BYTES = 56 * 1024 * 1024
LOGIT_CEIL_LOG2 = 40.0
LOGIT_BOUND_MAX = 80.0
LOGIT_DEAD = LOGIT_CEIL_LOG2 + 160.0

EV_QA, EV_KA, EV_VA, EV_QB, EV_KB, EV_VB, EV_QI, EV_MISC = 0, 512, 1024, 1536, 2048, 2560, 3072, 4096
EV_WIDTH = 4224
MISC_F = 64
MISC_W = 72

_NT = (((1,), (1,)), ((), ()))


def _params(n_axes, fused_inputs=None):
    return pltpu.CompilerParams(dimension_semantics=("arbitrary",) * n_axes,
                                vmem_limit_bytes=VMEM_LIMIT_BYTES, allow_input_fusion=fused_inputs)


def _rmsnorm_rows(x, g):
    ms = jnp.mean(x * x, axis=-1, keepdims=True)
    return x * lax.rsqrt(ms + EPS) * g


def _split3(x):
    hi = x.astype(BF16)
    r = x - hi.astype(F32)
    mid = r.astype(BF16)
    lo = (r - mid.astype(F32)).astype(BF16)
    return hi, mid, lo


def _rms_matmul_kernel(x_ref, g_ref, w_ref, o_ref):
    h = _rmsnorm_rows(x_ref[...], g_ref[...]).astype(BF16)
    o_ref[...] = jnp.dot(h, w_ref[...], preferred_element_type=F32)


def _rms_matmul(x, g, w, *, tm, tn):
    s, d = x.shape
    n = w.shape[1]
    return pl.pallas_call(
        _rms_matmul_kernel,
        grid=(n // tn, s // tm),
        in_specs=[pl.BlockSpec((tm, d), lambda j, i: (i, 0)),
                  pl.BlockSpec((1, d), lambda j, i: (0, 0)),
                  pl.BlockSpec((d, tn), lambda j, i: (0, j))],
        out_specs=pl.BlockSpec((tm, tn), lambda j, i: (i, j)),
        out_shape=jax.ShapeDtypeStruct((s, n), F32),
        compiler_params=_params(2, [False, False, True]),
        name="rms_matmul",
    )(x, g, w)


def _head_rmsnorm(x, gain, bmat):
    sq = x * x
    hi = sq.astype(BF16)
    lo = (sq - hi.astype(F32)).astype(BF16)
    ms = (jnp.dot(hi, bmat, preferred_element_type=F32)
          + jnp.dot(lo, bmat, preferred_element_type=F32))
    return x * lax.rsqrt(ms + EPS) * gain


def _rope(x, cos, sin_signed, low_half):
    rot = jnp.where(low_half, pltpu.roll(x, LANES - 32, 1), pltpu.roll(x, 32, 1))
    return x * cos + rot * sin_signed


def _log_sigmoid(z):
    return jnp.minimum(z, 0.0) - jnp.log1p(jnp.exp(-jnp.abs(z)))


def _prep_even_kernel(proj_ref, cos_ref, sin_ref, gqa_ref, gka_ref, gqb_ref, gkb_ref, bf_ref,
                      shift_ref, bmat_ref, tri_ref,
                      qa_ref, ka_ref, va_ref, qb_ref, kb_ref, vb_ref, qi_ref, ki_ref, wi_ref, cs_ref, carry):
    tm = proj_ref.shape[0]
    shift_a = shift_ref[0:1, :]
    shift_b = shift_ref[1:2, :]

    @pl.when(pl.program_id(0) == 0)
    def _():
        carry[...] = jnp.zeros_like(carry)

    cos = cos_ref[...]
    sin = sin_ref[...]
    bmat = bmat_ref[...]
    lane = lax.broadcasted_iota(jnp.int32, (tm, LANES), 1)
    low_half = (lane & 32) == 0
    head0 = lane < HEAD_DIM

    misc = proj_ref[:, EV_MISC:EV_MISC + LANES]

    logf = _log_sigmoid(misc + bf_ref[...])
    logf = jnp.where((lane >= MISC_F) & (lane < MISC_F + N_HEADS_A), logf, 0.0)
    tri = tri_ref[...]
    f_hi, f_mid, f_lo = _split3(logf)
    c = (jnp.dot(tri, f_hi, preferred_element_type=F32)
         + jnp.dot(tri, f_mid, preferred_element_type=F32)
         + jnp.dot(tri, f_lo, preferred_element_type=F32)) + carry[0:1, :]
    carry[0:1, :] = c[tm - 1:tm, :]
    cs_ref[...] = c * LOG2E
    c_hi, c_mid, c_lo = (t.astype(F32) for t in _split3(c * LOG2E))

    def place(block, h):
        return block if h % 2 == 0 else pltpu.roll(block, HEAD_DIM, 1)

    for hp in range(N_HEADS_A // 2):
        cols = slice(hp * LANES, (hp + 1) * LANES)
        qa = _head_rmsnorm(proj_ref[:, EV_QA + hp * LANES:EV_QA + (hp + 1) * LANES], gqa_ref[...], bmat)
        ka = _head_rmsnorm(proj_ref[:, EV_KA + hp * LANES:EV_KA + (hp + 1) * LANES], gka_ref[...], bmat)
        va = proj_ref[:, EV_VA + hp * LANES:EV_VA + (hp + 1) * LANES]
        qb = _rope(_head_rmsnorm(proj_ref[:, EV_QB + hp * LANES:EV_QB + (hp + 1) * LANES], gqb_ref[...], bmat),
                   cos, sin, low_half)
        kb = _rope(_head_rmsnorm(proj_ref[:, EV_KB + hp * LANES:EV_KB + (hp + 1) * LANES], gkb_ref[...], bmat),
                   cos, sin, low_half)
        vb = proj_ref[:, EV_VB + hp * LANES:EV_VB + (hp + 1) * LANES]
        del cols
        for sub in range(2):
            h = 2 * hp + sub
            src = MISC_F + h
            q_aug = jnp.where(
                head0, place(qa, h),
                jnp.where(lane == 64, pltpu.roll(c_hi, (64 - src) % LANES, 1),
                          jnp.where(lane == 65, pltpu.roll(c_mid, (65 - src) % LANES, 1),
                                    jnp.where(lane == 66, pltpu.roll(c_lo, (66 - src) % LANES, 1),
                                              jnp.where(lane < 70, 1.0,
                                                        jnp.where(lane == 70, shift_a, 0.0))))))
            k_aug = jnp.where(
                head0, place(ka, h),
                jnp.where(lane < 67, 1.0,
                          jnp.where(lane == 67, -pltpu.roll(c_hi, (67 - src) % LANES, 1),
                                    jnp.where(lane == 68, -pltpu.roll(c_mid, (68 - src) % LANES, 1),
                                              jnp.where(lane == 69, -pltpu.roll(c_lo, (69 - src) % LANES, 1),
                                                        jnp.where(lane == 70, 1.0, 0.0))))))
            ones_col = jnp.where(lane == HEAD_DIM, 1.0, 0.0)
            qa_ref[h] = q_aug.astype(BF16)
            ka_ref[h] = k_aug.astype(BF16)
            va_ref[h] = jnp.where(head0, place(va, h), ones_col).T.astype(BF16)
            qb_ref[h] = jnp.where(head0, place(qb, h), jnp.where(lane == HEAD_DIM, shift_b, 0.0)).astype(BF16)
            kb_ref[h] = jnp.where(head0, place(kb, h), ones_col).astype(BF16)
            vb_ref[h] = jnp.where(head0, place(vb, h), ones_col).T.astype(BF16)

    for blk in range(IDX_HEADS * IDX_DIM // LANES):
        cols = slice(EV_QI + blk * LANES, EV_QI + (blk + 1) * LANES)
        qi = _rope(proj_ref[:, cols], cos, sin, low_half) * (IDX_DIM ** -0.5)
        qi_ref[2 * blk] = qi[:, 0:IDX_DIM].astype(BF16)
        qi_ref[2 * blk + 1] = qi[:, IDX_DIM:2 * IDX_DIM].astype(BF16)
    ki_ref[...] = _rope(misc, cos, sin, low_half)[:, 0:IDX_DIM].astype(BF16)
    wi_ref[...] = (misc * (IDX_HEADS ** -0.5)).T


def _prep_even(proj, cos, sin, gqa, gka, gqb, gkb, bf_pad, shifts, bmat, tri, *, tm):
    s = proj.shape[0]
    row = lambda w: pl.BlockSpec((tm, w), lambda i: (i, 0))
    const = lambda a: pl.BlockSpec(a.shape, lambda i: (0, 0))
    heads = pl.BlockSpec((N_HEADS_A, tm, LANES), lambda i: (0, i, 0))
    heads_shape = jax.ShapeDtypeStruct((N_HEADS_A, s, LANES), BF16)
    heads_t = pl.BlockSpec((N_HEADS_A, LANES, tm), lambda i: (0, 0, i))
    heads_t_shape = jax.ShapeDtypeStruct((N_HEADS_A, LANES, s), BF16)
    return pl.pallas_call(
        _prep_even_kernel,
        grid=(s // tm,),
        in_specs=[row(EV_WIDTH), row(LANES), row(LANES), const(gqa), const(gka), const(gqb), const(gkb),
                  const(bf_pad), const(shifts), const(bmat), const(tri)],
        out_specs=[heads, heads, heads_t, heads, heads, heads_t,
                   pl.BlockSpec((IDX_HEADS, tm, IDX_DIM), lambda i: (0, i, 0)),
                   row(IDX_DIM),
                   pl.BlockSpec((LANES, tm), lambda i: (0, i)),
                   row(LANES)],
        out_shape=[heads_shape, heads_shape, heads_t_shape, heads_shape, heads_shape, heads_t_shape,
                   jax.ShapeDtypeStruct((IDX_HEADS, s, IDX_DIM), BF16),
                   jax.ShapeDtypeStruct((s, IDX_DIM), BF16),
                   jax.ShapeDtypeStruct((LANES, s), F32),
                   jax.ShapeDtypeStruct((s, LANES), F32)],
        scratch_shapes=[pltpu.VMEM((8, LANES), F32)],
        compiler_params=_params(1),
        name="prep_even",
    )(proj, cos, sin, gqa, gka, gqb, gkb, bf_pad, shifts, bmat, tri)


def _float_key(x):
    bits = pltpu.bitcast(x, jnp.int32)
    return bits ^ ((bits >> 31) & 0x7FFFFFFF)


def _key_float(k):
    return pltpu.bitcast(k ^ ((k >> 31) & 0x7FFFFFFF), F32)


def _kth_search(count_fn, lo_k, hi_k, c_lo, done, kf, *, fixed_steps=None, steps_per_check=1,
                stop_on_count=True):
    def finished(lo, hi, c, d):
        d = d | (lo + 1 >= hi)
        return jnp.where(d | (c == kf) if stop_on_count else d, 1.0, 0.0)

    def halve(lo, hi, c, d):
        live = d == 0.0
        cand = (lo >> 1) + (hi >> 1) + (lo & hi & 1)
        n = count_fn(_key_float(cand))
        take = (n >= kf) & live
        drop = (n < kf) & live
        lo = jnp.where(take, cand, lo)
        c = jnp.where(take, n, c)
        hi = jnp.where(drop, cand, hi)
        return lo, hi, c, finished(lo, hi, c, d > 0.0)

    d0 = finished(lo_k, hi_k, c_lo, done)
    if fixed_steps is not None:
        lo_k, _, c_lo, _ = lax.fori_loop(0, fixed_steps, lambda _, st: halve(*st), (lo_k, hi_k, c_lo, d0))
        return lo_k, c_lo

    def body(state):
        st = state[:4]
        for _ in range(steps_per_check):
            st = halve(*st)
        return (*st, jnp.max(1.0 - st[3]))

    lo_k, _, c_lo, _, _ = lax.while_loop(lambda state: state[4] > 0.0, body,
                                         (lo_k, hi_k, c_lo, d0, jnp.max(1.0 - d0)))
    return lo_k, c_lo


def _indexer_kernel(qi_ref, ki_ref, wi_ref, bias_ref, sc_scr, g_scr, *, tq, tk, topk, group):
    i = pl.program_id(0)
    s_keys = sc_scr.shape[0]
    n_chunks = s_keys // tk
    nk = ((i + 1) * tq + tk - 1) // tk
    sub = 8
    g_rows = tk // group
    heads_per_dot = max(1, MXU_WIDTH // tq)
    kf = float(topk)
    lowest = float(np.finfo(np.float32).min)
    qpos = lax.broadcasted_iota(jnp.int32, (1, tq), 1) + i * tq
    w = wi_ref[...]

    g_scr[...] = jnp.full(g_scr.shape, NEG_INF, F32)

    def score_chunk(c, gmax):
        ks = pl.multiple_of(c * tk, tk)
        ki = ki_ref[pl.ds(ks, tk), :]
        acc = jnp.zeros((tk, tq), F32)
        for first_head in range(0, IDX_HEADS, heads_per_dot):
            q_rows = [qi_ref[first_head + u] for u in range(heads_per_dot)]
            q_cat = q_rows[0] if heads_per_dot == 1 else jnp.concatenate(q_rows, axis=0)
            logits = lax.dot_general(ki, q_cat, _NT, preferred_element_type=F32)
            for u in range(heads_per_dot):
                w_h = w[MISC_W + first_head + u:MISC_W + first_head + u + 1, :]
                acc = acc + w_h * jnp.maximum(logits[:, u * tq:(u + 1) * tq], 0.0)
        kpos = lax.broadcasted_iota(jnp.int32, (tk, tq), 0) + c * tk
        sc = jnp.where(kpos <= qpos, acc, NEG_INF)
        sc_scr[pl.ds(ks, tk), :] = sc
        g = jnp.max(sc.reshape(tk // (group * sub), group, sub, tq), axis=1).reshape(g_rows, tq)
        g_scr[pl.ds(pl.multiple_of(c * g_rows, g_rows), g_rows), :] = g
        return jnp.maximum(gmax, jnp.max(g, axis=0, keepdims=True))

    gmax = lax.fori_loop(0, nk, score_chunk, jnp.full((1, tq), NEG_INF, F32))

    def count_rows(ref, rows_per_step, n_steps, cand):
        ways = min(8, rows_per_step // sub)

        def step(c, cnt):
            r0 = pl.multiple_of(c * rows_per_step, rows_per_step)
            ones = jnp.where(ref[pl.ds(r0, rows_per_step), :] >= cand, 1.0, 0.0)
            return cnt + jnp.sum(ones.reshape(rows_per_step // (ways * sub), ways, sub, tq), axis=0)

        cnt = lax.fori_loop(0, n_steps, step, jnp.zeros((ways, sub, tq), F32))
        return jnp.sum(jnp.sum(cnt, axis=0), axis=0, keepdims=True)

    g_piece = min(256, g_scr.shape[0])
    count_groups = lambda cand: count_rows(g_scr, g_piece, (nk * g_rows + g_piece - 1) // g_piece, cand)
    count_all = lambda cand: count_rows(sc_scr, tk, nk, cand)

    lo0 = _float_key(jnp.full((1, tq), lowest, F32))
    hi0 = _float_key(gmax) + 1
    few = qpos < topk
    c_groups = count_groups(jnp.full((1, tq), lowest, F32))
    u_k, _ = _kth_search(count_groups, lo0, hi0, c_groups, few | (c_groups < kf), kf, fixed_steps=32)
    u = _key_float(u_k)
    c_u = count_all(u)
    n_max = count_groups(u)
    rank = kf - (c_u - n_max)
    has_y = jnp.logical_not(few) & (rank >= 1.0) & (rank <= n_max)
    y_k, _ = _kth_search(count_groups, u_k, hi0, n_max, jnp.logical_not(has_y), rank, fixed_steps=32,
                         stop_on_count=False)
    hi1 = jnp.where(has_y, y_k + 1, hi0)
    t_k, _ = _kth_search(count_all, u_k, hi1, c_u, few, kf, steps_per_check=2)
    thr = jnp.where(few, lowest, _key_float(t_k))

    def bias_chunk(c, carry):
        ks = pl.multiple_of(c * tk, tk)
        bias_ref[pl.ds(ks, tk), :] = jnp.where(sc_scr[pl.ds(ks, tk), :] >= thr, 1.0, 0.0).astype(BF16)
        return carry

    lax.fori_loop(0, nk, bias_chunk, 0)

    def fill_chunk(c, carry):
        bias_ref[pl.ds(pl.multiple_of(c * tk, tk), tk), :] = jnp.zeros((tk, tq), BF16)
        return carry

    lax.fori_loop(nk, n_chunks, fill_chunk, 0)


def _indexer(qi, ki, wi_t, *, tq, tk, topk, group):
    s = ki.shape[0]
    return pl.pallas_call(
        functools.partial(_indexer_kernel, tq=tq, tk=tk, topk=topk, group=group),
        grid=(s // tq,),
        in_specs=[pl.BlockSpec((IDX_HEADS, tq, IDX_DIM), lambda i: (0, i, 0)),
                  pl.BlockSpec((s, IDX_DIM), lambda i: (0, 0)),
                  pl.BlockSpec((LANES, tq), lambda i: (0, i))],
        out_specs=pl.BlockSpec((None, s, tq), lambda i: (i, 0, 0)),
        out_shape=jax.ShapeDtypeStruct((s // tq, s, tq), BF16),
        scratch_shapes=[pltpu.VMEM((s, tq), F32), pltpu.VMEM((s // group, tq), F32)],
        compiler_params=_params(1),
        name="dsa_indexer",
    )(qi, ki, wi_t)


def _attn_kernel(*refs, has_mask, bounded):
    first_ref, q_ref, k_ref, vt_ref = refs[:4]
    refs = refs[1:]
    mask_ref = refs[3] if has_mask else None
    o_ref = refs[3 + has_mask]
    scratch = refs[4 + has_mask:]
    acc_scr = scratch[-1]
    m_scr = None if bounded else scratch[0]
    n_heads, t, _ = q_ref.shape
    i = pl.program_id(0)
    j = pl.program_id(1)

    @pl.when(j == 0)
    def _():
        acc_scr[...] = jnp.zeros(acc_scr.shape, F32)
        if not bounded:
            m_scr[...] = jnp.full(m_scr.shape, NEG_INF, F32)

    def step(diagonal):
        if has_mask:
            mask = jnp.concatenate([mask_ref[b] for b in range(mask_ref.shape[0])], axis=1)
            if not bounded:
                selected = mask.astype(F32) > 0.0
        if diagonal:
            causal = (lax.broadcasted_iota(jnp.int32, (t, t), 0)
                      <= lax.broadcasted_iota(jnp.int32, (t, t), 1))

        def logits(h):
            return lax.dot_general(k_ref[h], q_ref[h], _NT, preferred_element_type=F32)

        s_next = logits(0)
        for h in range(n_heads):
            s = s_next
            if h + 1 < n_heads:
                s_next = logits(h + 1)
            if diagonal:
                s = jnp.where(causal, s, NEG_INF)
            if bounded:
                p = jnp.exp2(s).astype(BF16)
                if has_mask:
                    p = p * mask
                acc_scr[h] = acc_scr[h] + jnp.dot(vt_ref[h], p, preferred_element_type=F32)
            else:
                if has_mask:
                    s = jnp.where(selected, s, NEG_INF)
                m_old = m_scr[h][0:1, :]
                m_new = jnp.maximum(m_old, jnp.max(s, axis=0, keepdims=True))
                m_safe = jnp.where(m_new == NEG_INF, 0.0, m_new)
                p = jnp.exp2(s - m_safe)
                alpha = jnp.exp2(m_old - m_safe)
                acc_scr[h] = alpha * acc_scr[h] + jnp.dot(vt_ref[h], p.astype(BF16), preferred_element_type=F32)
                m_scr[h] = jnp.broadcast_to(m_new, m_scr.shape[1:])

    @pl.when((j < i) & (j >= first_ref[i]))
    def _():
        step(False)

    @pl.when(j == i)
    def _():
        step(True)
        for hp in range(n_heads // 2):
            halves = []
            for h in (2 * hp, 2 * hp + 1):
                a = acc_scr[h]
                halves.append(a[0:HEAD_DIM, :] / a[HEAD_DIM:HEAD_DIM + 1, :])
            o_ref[:, hp * LANES:(hp + 1) * LANES] = jnp.concatenate(halves, axis=0).T.astype(o_ref.dtype)


def _attention(q, k, vt, mask, first, *, t, bounded):
    n_heads, s, _ = q.shape
    sub = 8
    key_tile = lambda i, j, first_ref: jnp.clip(j, first_ref[i], i)
    in_specs = [pl.BlockSpec((n_heads, t, LANES), lambda i, j, f: (0, i, 0)),
                pl.BlockSpec((n_heads, t, LANES), lambda i, j, f: (0, key_tile(i, j, f), 0)),
                pl.BlockSpec((n_heads, LANES, t), lambda i, j, f: (0, 0, key_tile(i, j, f)))]
    args = [q, k, vt]
    if mask is not None:
        tq_mask = mask.shape[2]
        in_specs.append(pl.BlockSpec((t // tq_mask, t, tq_mask), lambda i, j, f: (i, key_tile(i, j, f), 0)))
        args.append(mask)
    scratch = [pltpu.VMEM((n_heads, LANES, t), F32)]
    if not bounded:
        scratch.insert(0, pltpu.VMEM((n_heads, sub, t), F32))
    name = ("dsa_attention" if mask is not None else "fox_attention") + ("" if bounded else "_online")
    return pl.pallas_call(
        functools.partial(_attn_kernel, has_mask=mask is not None, bounded=bounded),
        grid_spec=pltpu.PrefetchScalarGridSpec(
            num_scalar_prefetch=1,
            grid=(s // t, s // t),
            in_specs=in_specs,
            out_specs=pl.BlockSpec((t, n_heads * HEAD_DIM), lambda i, j, f: (i, 0)),
            scratch_shapes=scratch),
        out_shape=jax.ShapeDtypeStruct((s, n_heads * HEAD_DIM), BF16),
        compiler_params=_params(2),
        name=name,
    )(first, *args)


def _attention_any(q, k, vt, mask, logit_bound, first, *, t):
    operands = (q, k, vt, first) if mask is None else (q, k, vt, first, mask)
    m = lambda ops: ops[4] if mask is not None else None
    return lax.cond(logit_bound <= LOGIT_BOUND_MAX,
                    lambda *ops: _attention(ops[0], ops[1], ops[2], m(ops), ops[3], t=t, bounded=True),
                    lambda *ops: _attention(ops[0], ops[1], ops[2], m(ops), jnp.zeros_like(ops[3]), t=t,
                                            bounded=False),
                    *operands)


def _first_live_tile(cs, t):
    top = cs[0::t]
    bottom = cs[t - 1::t]
    live = jnp.any(top[:, None, :] - bottom[None, :, :] > -LOGIT_DEAD, axis=-1)
    return jnp.argmax(live, axis=1).astype(jnp.int32)


def _matmul_res_kernel(*refs, n_pairs):
    res_ref = refs[2 * n_pairs]
    o_ref = refs[2 * n_pairs + 1]
    acc = res_ref[...]
    for p in range(n_pairs):
        acc = acc + jnp.dot(refs[2 * p][...], refs[2 * p + 1][...], preferred_element_type=F32)
    o_ref[...] = acc


def _matmul_res(pairs, res, *, tm):
    s, n = res.shape
    in_specs, args = [], []
    for a, w in pairs:
        in_specs += [pl.BlockSpec((tm, a.shape[1]), lambda i: (i, 0)),
                     pl.BlockSpec(w.shape, lambda i: (0, 0))]
        args += [a, w]
    in_specs.append(pl.BlockSpec((tm, n), lambda i: (i, 0)))
    args.append(res)
    return pl.pallas_call(
        functools.partial(_matmul_res_kernel, n_pairs=len(pairs)),
        grid=(s // tm,),
        in_specs=in_specs,
        out_specs=pl.BlockSpec((tm, n), lambda i: (i, 0)),
        out_shape=jax.ShapeDtypeStruct((s, n), F32),
        compiler_params=_params(1, [False, True] * len(pairs) + [False]),
        name="matmul_residual",
    )(*args)


def _prep_odd_kernel(proj_ref, cos_ref, sin_ref, gq_ref, gk_ref, bmat_ref, q_ref, k_ref, v_ref):
    tm = proj_ref.shape[0]
    cos = cos_ref[...]
    sin = sin_ref[...]
    bmat = bmat_ref[...]
    lane = lax.broadcasted_iota(jnp.int32, (tm, LANES), 1)
    low_half = (lane & 32) == 0
    for blk in range(D_MODEL // LANES):
        cols = slice(blk * LANES, (blk + 1) * LANES)
        q = proj_ref[:, blk * LANES:(blk + 1) * LANES]
        k = proj_ref[:, D_MODEL + blk * LANES:D_MODEL + (blk + 1) * LANES]
        q_ref[:, cols] = _rope(_head_rmsnorm(q, gq_ref[...], bmat), cos, sin, low_half)
        k_ref[:, cols] = _rope(_head_rmsnorm(k, gk_ref[...], bmat), cos, sin, low_half)
        v_ref[:, cols] = proj_ref[:, 2 * D_MODEL + blk * LANES:2 * D_MODEL + (blk + 1) * LANES]


def _prep_odd(proj, cos, sin, gq, gk, bmat, *, tm):
    s = proj.shape[0]
    row = lambda w: pl.BlockSpec((tm, w), lambda i: (i, 0))
    const = lambda a: pl.BlockSpec(a.shape, lambda i: (0, 0))
    shape = jax.ShapeDtypeStruct((s, D_MODEL), F32)
    return pl.pallas_call(
        _prep_odd_kernel,
        grid=(s // tm,),
        in_specs=[row(3 * D_MODEL), row(LANES), row(LANES), const(gq), const(gk), const(bmat)],
        out_specs=[row(D_MODEL)] * 3,
        out_shape=[shape] * 3,
        compiler_params=_params(1),
        name="prep_odd",
    )(proj, cos, sin, gq, gk, bmat)


def _band_head_pair(q2, k2, v2, ok, first, first2, shift):
    zero = jnp.zeros((), BF16)
    one = jnp.ones((), BF16)
    res, mx = [], []
    for sub in range(2):
        mine = first if sub == 0 else jnp.logical_not(first)
        mine2 = first2 if sub == 0 else jnp.logical_not(first2)
        s = lax.dot_general(jnp.where(mine, q2, zero), k2, _NT, preferred_element_type=F32)
        if shift is None:
            s = jnp.where(ok, s, NEG_INF)
            m = jnp.max(s, axis=-1, keepdims=True)
            p = jnp.exp2(s - m).astype(BF16)
            mx.append(m)
        else:
            p = jnp.exp2(jnp.where(ok, s + shift, NEG_INF)).astype(BF16)
        res.append(jnp.dot(p, jnp.where(mine2, v2, one), preferred_element_type=F32))
    num = jnp.where(first, res[0], res[1])
    den_sw = jnp.where(first, res[1], res[0])
    den = pltpu.roll(den_sw, HEAD_DIM, 1)
    if shift is not None:
        return num, den
    return num / den, jnp.where(first, mx[0], mx[1]) + jnp.log2(den)


def _dilated_kernel(shift_ref, q_ref, kc_ref, kp_ref, vc_ref, vp_ref, o_ref, o_scr, l_scr, *, bounded):
    shift = shift_ref[0] if bounded else None
    win = q_ref.shape[0]
    t = Q_BLOCK
    qi = lax.broadcasted_iota(jnp.int32, (t, t), 0)
    kj = lax.broadcasted_iota(jnp.int32, (t, t), 1)
    ok_inner = jnp.concatenate([kj >= qi, kj <= qi], axis=1)
    ok_first = jnp.concatenate([(kj >= qi) & (pl.program_id(0) > 0), kj <= qi], axis=1)
    first = lax.broadcasted_iota(jnp.int32, (t, LANES), 1) < HEAD_DIM
    first2 = lax.broadcasted_iota(jnp.int32, (2 * t, LANES), 1) < HEAD_DIM

    for p, (_, dil) in enumerate(DIL_PATTERNS):
        span = dil * t
        n_blocks = win // span
        for r in range(dil):
            for cb in range(n_blocks):
                rows = pl.ds(r + cb * span, t, stride=dil) if dil > 1 else pl.ds(cb * t, t)
                if cb > 0:
                    prows = pl.ds(r + (cb - 1) * span, t, stride=dil) if dil > 1 else pl.ds((cb - 1) * t, t)
                    k_prev, v_prev, ok = kc_ref[prows, :], vc_ref[prows, :], ok_inner
                else:
                    prows = (pl.ds(r + (n_blocks - 1) * span, t, stride=dil) if dil > 1
                             else pl.ds((n_blocks - 1) * t, t))
                    k_prev, v_prev, ok = kp_ref[prows, :], vp_ref[prows, :], ok_first
                k2 = jnp.concatenate([k_prev, kc_ref[rows, :]], axis=0).astype(BF16)
                v2 = jnp.concatenate([v_prev, vc_ref[rows, :]], axis=0).astype(BF16)
                o2, l2 = _band_head_pair(q_ref[rows, :].astype(BF16), k2, v2, ok, first, first2, shift)
                o_scr[p, rows, :] = o2
                l_scr[p, rows, :] = l2

    if bounded:
        merged = (o_scr[0] + o_scr[1] + o_scr[2]) / (l_scr[0] + l_scr[1] + l_scr[2])
    else:
        l1, l2, l3 = l_scr[0], l_scr[1], l_scr[2]
        m = jnp.maximum(jnp.maximum(l1, l2), l3)
        w1, w2, w3 = jnp.exp2(l1 - m), jnp.exp2(l2 - m), jnp.exp2(l3 - m)
        merged = (w1 * o_scr[0] + w2 * o_scr[1] + w3 * o_scr[2]) / (w1 + w2 + w3)
    o_ref[...] = merged.astype(o_ref.dtype)


def _dilated(q, k, v, shift, *, bounded):
    s = q.shape[0]
    win = max(dil for _, dil in DIL_PATTERNS) * Q_BLOCK
    assert s % win == 0
    cur = pl.BlockSpec((win, LANES), lambda c, j: (c, j))
    prev = pl.BlockSpec((win, LANES), lambda c, j: (jnp.maximum(c - 1, 0), j))
    return pl.pallas_call(
        functools.partial(_dilated_kernel, bounded=bounded),
        grid=(s // win, D_MODEL // LANES),
        in_specs=[pl.BlockSpec(memory_space=pltpu.SMEM), cur, cur, prev, cur, prev],
        out_specs=cur,
        out_shape=jax.ShapeDtypeStruct((s, D_MODEL), BF16),
        scratch_shapes=[pltpu.VMEM((len(DIL_PATTERNS), win, LANES), F32),
                        pltpu.VMEM((len(DIL_PATTERNS), win, LANES), F32)],
        compiler_params=_params(2),
        name="dilated_attention" if bounded else "dilated_attention_online",
    )(shift, q, k, k, v, v)


def _dilated_any(q, k, v, logit_bound):
    shift = jnp.reshape(LOGIT_CEIL_LOG2 - logit_bound, (1,)).astype(F32)
    return lax.cond(logit_bound <= LOGIT_BOUND_MAX,
                    lambda *ops: _dilated(*ops, bounded=True),
                    lambda *ops: _dilated(*ops, bounded=False),
                    q, k, v, shift)


def _gelu_tanh(x):
    return 0.5 * x * (1.0 + jnp.tanh(np.sqrt(2.0 / np.pi).astype(np.float32) * (x + 0.044715 * (x * x * x))))


def _ffn_act_kernel(x_ref, halo_ref, g_ref, wg_ref, wu_ref, cw_ref, cb_ref, o_ref):
    i = pl.program_id(1)
    tm = x_ref.shape[0]
    h = _rmsnorm_rows(x_ref[...], g_ref[...]).astype(BF16)
    hh = _rmsnorm_rows(halo_ref[...], g_ref[...]).astype(BF16)
    wg = wg_ref[...]
    gate = jnp.dot(h, wg, preferred_element_type=F32)
    gate_halo = jnp.where(i > 0, jnp.dot(hh, wg, preferred_element_type=F32), 0.0)
    up = jnp.dot(h, wu_ref[...], preferred_element_type=F32)
    row = lax.broadcasted_iota(jnp.int32, gate.shape, 0)
    g_m1 = jnp.where(row == 0, gate_halo[7:8, :], pltpu.roll(gate, 1, 0))
    g_m2 = jnp.where(row == 0, gate_halo[6:7, :],
                     jnp.where(row == 1, gate_halo[7:8, :], pltpu.roll(gate, 2, 0)))
    cw = cw_ref[...]
    conv = cw[2:3, :] * gate + cw[1:2, :] * g_m1 + cw[0:1, :] * g_m2 + cb_ref[...]
    o_ref[...] = (_gelu_tanh(conv) * up).astype(o_ref.dtype)
    del tm


def _ffn_act(x, g, wg, wu, cw, cb, *, tm, tn):
    s, d = x.shape
    n = wg.shape[1]
    halo_rows = 8
    return pl.pallas_call(
        _ffn_act_kernel,
        grid=(n // tn, s // tm),
        in_specs=[pl.BlockSpec((tm, d), lambda j, i: (i, 0)),
                  pl.BlockSpec((halo_rows, d), lambda j, i: (jnp.maximum(i * (tm // halo_rows) - 1, 0), 0)),
                  pl.BlockSpec((1, d), lambda j, i: (0, 0)),
                  pl.BlockSpec((d, tn), lambda j, i: (0, j)),
                  pl.BlockSpec((d, tn), lambda j, i: (0, j)),
                  pl.BlockSpec((cw.shape[0], tn), lambda j, i: (0, j)),
                  pl.BlockSpec((1, tn), lambda j, i: (0, j))],
        out_specs=pl.BlockSpec((tm, tn), lambda j, i: (i, j)),
        out_shape=jax.ShapeDtypeStruct((s, n), BF16),
        compiler_params=_params(2, [False, False, False, True, True, False, False]),
        name="ffn_act",
    )(x, x, g, wg, wu, cw, cb)


def _ple_kernel(x_ref, p_ref, g_ref, wg_ref, wp_ref, o_ref):
    x = x_ref[...]
    z = jnp.dot(_rmsnorm_rows(x, g_ref[...]).astype(BF16), wg_ref[...], preferred_element_type=F32)
    gate = 1.0 / (1.0 + jnp.exp(-z))
    proj = jnp.dot(p_ref[...].astype(BF16), wp_ref[...], preferred_element_type=F32)
    o_ref[...] = x + gate * proj


def _ple(x, p, g, wg, wp, *, tm):
    s, d = x.shape
    return pl.pallas_call(
        _ple_kernel,
        grid=(s // tm,),
        in_specs=[pl.BlockSpec((tm, d), lambda i: (i, 0)),
                  pl.BlockSpec((tm, p.shape[1]), lambda i: (i, 0)),
                  pl.BlockSpec((1, d), lambda i: (0, 0)),
                  pl.BlockSpec(wg.shape, lambda i: (0, 0)),
                  pl.BlockSpec(wp.shape, lambda i: (0, 0))],
        out_specs=pl.BlockSpec((tm, d), lambda i: (i, 0)),
        out_shape=jax.ShapeDtypeStruct((s, d), F32),
        compiler_params=_params(1, [False, False, False, True, True]),
        name="ple",
    )(x, p, g, wg, wp)


def _rope_tables(s):
    half = HEAD_DIM // 2
    inv_freq = ROPE_THETA ** (-jnp.arange(half, dtype=F32) / half)
    ang = jnp.arange(s).astype(F32)[:, None] * inv_freq[None, :]
    cos, sin = jnp.cos(ang), jnp.sin(ang)
    return jnp.tile(cos, (1, 4)), jnp.tile(jnp.concatenate([-sin, sin], axis=1), (1, 2))


def _logit_bound(gq, gk):
    return (1.02 * HEAD_DIM * HEAD_DIM ** -0.5 * LOG2E) * jnp.max(jnp.abs(gq)) * jnp.max(jnp.abs(gk))


def _gain_tile(g, scale=1.0):
    return (jnp.tile(g.astype(F32), 2) * scale)[None, :]


def _even_w_in(w):
    hd = N_HEADS_A * HEAD_DIM
    cuts = np.cumsum([hd, hd, hd, N_HEADS_A, hd, hd, hd, IDX_HEADS * IDX_DIM, IDX_DIM, IDX_HEADS])[:-1]
    qa, ka, va, fa, qb, kb, vb, qi, ki, wi = jnp.split(w, [int(c) for c in cuts], axis=1)
    pad = jnp.zeros((w.shape[0], LANES - IDX_DIM - N_HEADS_A - IDX_HEADS), w.dtype)
    return jnp.concatenate([qa, ka, va, qb, kb, vb, qi, ki, fa, wi, pad], axis=1).astype(BF16)


def kernel(x, p, norm_mix, w_in_even, b_forget, q_norm_a, k_norm_a, q_norm_b, k_norm_b, w_in_odd, q_norm_c,
           k_norm_c, w_out, norm_ffn, w_gate, w_up, conv_w, conv_b, w_down, norm_ple, w_ple_gate, w_ple_proj):
    batch, s, d = x.shape
    assert batch == 1 and d == D_MODEL
    depth = norm_mix.shape[0]
    topk = min(TOPK_MAX, s // 4)
    t_attn = min(512, s)
    tm = min(512, s)
    tm_prep = min(256, s)
    qscale = HEAD_DIM ** -0.5 * LOG2E

    cos, sin = _rope_tables(s)
    lane = np.arange(LANES)
    bmat = jnp.asarray((lane[:, None] // HEAD_DIM == lane[None, :] // HEAD_DIM) / HEAD_DIM, BF16)
    tri = jnp.asarray(np.tril(np.ones((tm_prep, tm_prep))), BF16)

    xs = x[0]
    for i in range(depth):
        j = i // 2
        g_mix = norm_mix[i][None, :]
        w_o = w_out[i].astype(BF16)
        if i % 2 == 0:
            proj = _rms_matmul(xs, g_mix, _even_w_in(w_in_even[j]), tm=tm, tn=EV_WIDTH // 3)
            bf_pad = jnp.zeros((1, LANES), F32).at[0, MISC_F:MISC_F + N_HEADS_A].set(b_forget[j])
            bound_a = _logit_bound(q_norm_a[j], k_norm_a[j])
            bound_b = _logit_bound(q_norm_b[j], k_norm_b[j])
            shifts = jnp.stack([jnp.full((LANES,), LOGIT_CEIL_LOG2 - bound_a, F32),
                                jnp.full((LANES,), LOGIT_CEIL_LOG2 - bound_b, F32)])
            qa, ka, va_t, qb, kb, vb_t, qi, ki, wi_t, cs = _prep_even(
                proj, cos, sin, _gain_tile(q_norm_a[j], qscale), _gain_tile(k_norm_a[j]),
                _gain_tile(q_norm_b[j], qscale), _gain_tile(k_norm_b[j]), bf_pad, shifts, bmat, tri,
                tm=tm_prep)
            first_a = _first_live_tile(cs[:, MISC_F:MISC_F + N_HEADS_A], t_attn)
            out_a = _attention_any(qa, ka, va_t, None, bound_a, first_a, t=t_attn)
            mask = _indexer(qi, ki, wi_t, tq=min(2 * LANES, s), tk=t_attn, topk=topk, group=16)
            out_b = _attention_any(qb, kb, vb_t, mask, bound_b, jnp.zeros_like(first_a), t=t_attn)
            half = N_HEADS_A * HEAD_DIM
            xs = _matmul_res([(out_a, w_o[:half]), (out_b, w_o[half:])], xs, tm=tm)
        else:
            proj = _rms_matmul(xs, g_mix, w_in_odd[j].astype(BF16), tm=tm, tn=D_MODEL)
            q, k, v = _prep_odd(proj, cos, sin, _gain_tile(q_norm_c[j], qscale), _gain_tile(k_norm_c[j]),
                                bmat, tm=tm_prep)
            merged = _dilated_any(q, k, v, _logit_bound(q_norm_c[j], k_norm_c[j]))
            xs = _matmul_res([(merged, w_o)], xs, tm=tm)
        act = _ffn_act(xs, norm_ffn[i][None, :], w_gate[i].astype(BF16), w_up[i].astype(BF16),
                       conv_w[i], conv_b[i][None, :], tm=tm, tn=D_FF // 2)
        xs = _matmul_res([(act, w_down[i].astype(BF16))], xs, tm=tm)
        xs = _ple(xs, p[i, 0], norm_ple[i][None, :], w_ple_gate[i].astype(BF16),
                  w_ple_proj[i].astype(BF16), tm=tm)
    return xs[None]
```
